```python
import jax, jax.numpy as jnp
from jax import lax
import numpy as np

D_MODEL = 2048
BATCH = 4
SEQ = 8192
DEPTH = 1

ATTN_WIDTH = D_MODEL // 2
N_HEADS = 8
HEAD_DIM = ATTN_WIDTH // N_HEADS
MOBA_BLOCK = 256
MOBA_TOPK = 3
Q_CHUNK = 32
POOL_WIDTH = D_MODEL - ATTN_WIDTH
POOL_WINDOWS = (2, 4, 8, 16)
N_POOL_GROUPS = len(POOL_WINDOWS)
POOL_GROUP = POOL_WIDTH // N_POOL_GROUPS
MIX_WIDTH = ATTN_WIDTH + POOL_WIDTH
IN_WIDTH = 3 * ATTN_WIDTH + POOL_WIDTH
N_GROUPS = 8
EXPERTS_PER_GROUP = 8
N_EXPERTS = N_GROUPS * EXPERTS_PER_GROUP
TOPK_INNER = 2
D_EXPERT = D_MODEL // 4
DISPATCH_BLOCK = 128
PLE_DIM = 256
EPS = 1e-6

kernel_name = 'hymba_moba_pool_hmoe_ple'


def rms_norm(x, gain):
    xf = x.astype(jnp.float32)
    y = xf * lax.rsqrt(jnp.mean(xf * xf, axis=-1, keepdims=True) + EPS)
    return (y * gain.astype(jnp.float32)).astype(x.dtype)


def alibi_slopes(n):
    return jnp.asarray([2.0 ** (-8.0 * (h + 1) / n) for h in range(n)], jnp.float32)


def moba_attention(q, k, v):
    B, S = q.shape[0], q.shape[1]
    nb = -(-S // MOBA_BLOCK)
    pad = nb * MOBA_BLOCK - S
    K = min(MOBA_TOPK, nb)
    q = q.transpose(0, 2, 1, 3)
    k = jnp.pad(k.transpose(0, 2, 1, 3), ((0, 0), (0, 0), (0, pad), (0, 0)))
    v = jnp.pad(v.transpose(0, 2, 1, 3), ((0, 0), (0, 0), (0, pad), (0, 0)))
    kb = k.reshape(B, N_HEADS, nb, MOBA_BLOCK, HEAD_DIM)
    vb = v.reshape(B, N_HEADS, nb, MOBA_BLOCK, HEAD_DIM)
    kmean = jnp.mean(kb.astype(jnp.float32), axis=3).astype(k.dtype)
    slopes = alibi_slopes(N_HEADS)[None, :, None, None]
    scale = HEAD_DIM ** -0.5
    bi = jnp.arange(B)[:, None, None, None]
    hi = jnp.arange(N_HEADS)[None, :, None, None]
    blk_ids = jnp.arange(nb)
    in_blk = jnp.arange(MOBA_BLOCK)
    rank = jnp.arange(K)

    def chunk(c):
        start = c * Q_CHUNK
        own = start // MOBA_BLOCK
        qc = lax.dynamic_slice_in_dim(q, start, Q_CHUNK, axis=2)
        qpos = start + jnp.arange(Q_CHUNK)
        gate = jnp.einsum('bhqd,bhnd->bhqn', qc, kmean).astype(jnp.float32)
        gate = jnp.where(blk_ids < own, gate, -jnp.inf)
        _, sel = lax.top_k(gate, K)
        sel_ok = rank < own
        k_sel = kb[bi, hi, sel]
        v_sel = vb[bi, hi, sel]
        s_sel = jnp.einsum('bhqd,bhqnkd->bhqnk', qc, k_sel).astype(jnp.float32) * scale
        kpos_sel = sel[..., None] * MOBA_BLOCK + in_blk
        dist_sel = (qpos[:, None, None] - kpos_sel).astype(jnp.float32)
        s_sel = s_sel - slopes[..., None] * dist_sel
        s_sel = jnp.where(sel_ok[:, None], s_sel, -jnp.inf)
        k_own = lax.dynamic_index_in_dim(kb, own, axis=2, keepdims=False)
        v_own = lax.dynamic_index_in_dim(vb, own, axis=2, keepdims=False)
        s_own = jnp.einsum('bhqd,bhkd->bhqk', qc, k_own).astype(jnp.float32) * scale
        dist_own = qpos[:, None] - (own * MOBA_BLOCK + in_blk)[None, :]
        s_own = jnp.where(dist_own >= 0, s_own - slopes * dist_own.astype(jnp.float32), -jnp.inf)
        s = jnp.concatenate([s_sel.reshape(B, N_HEADS, Q_CHUNK, K * MOBA_BLOCK), s_own], axis=-1)
        prob = jax.nn.softmax(s, axis=-1).astype(v.dtype)
        p_sel = prob[..., :K * MOBA_BLOCK].reshape(B, N_HEADS, Q_CHUNK, K, MOBA_BLOCK)
        p_own = prob[..., K * MOBA_BLOCK:]
        return (jnp.einsum('bhqnk,bhqnkd->bhqd', p_sel, v_sel)
                + jnp.einsum('bhqk,bhkd->bhqd', p_own, v_own))

    outs = lax.map(chunk, jnp.arange(S // Q_CHUNK))
    return outs.transpose(1, 0, 3, 2, 4).reshape(B, S, N_HEADS * HEAD_DIM)


def pool_mixer(u, w_pool):
    B, S, _ = u.shape
    uf = u.astype(jnp.float32).reshape(B, S, N_POOL_GROUPS, POOL_GROUP)
    csum = jnp.concatenate([jnp.zeros_like(uf[:, :1]), jnp.cumsum(uf, axis=1)], axis=1)
    t = jnp.arange(S)[:, None]
    win = jnp.asarray(POOL_WINDOWS, jnp.int32)[None, :]
    lo = jnp.maximum(t + 1 - win, 0)
    gi = jnp.arange(N_POOL_GROUPS)[None, :]
    window_sum = csum[:, 1:] - csum[:, lo, gi]
    count = jnp.minimum(t + 1, win).astype(jnp.float32)
    z = window_sum / count[None, :, :, None] - uf
    z = jnp.einsum('bsgc,gcd->bsgd', z.astype(u.dtype), w_pool)
    return z.reshape(B, S, POOL_WIDTH)


def hier_moe(xn, w_rg, b_rg, w_re, b_re, w_gate, w_up, w_down):
    B, S, D = xn.shape
    T = B * S
    xt = xn.reshape(T, D)
    g_prob = jax.nn.softmax((xt @ w_rg).astype(jnp.float32) + b_rg.astype(jnp.float32), axis=-1)
    g_val, g_idx = lax.top_k(g_prob, 1)
    g_val, g_idx = g_val[:, 0], g_idx[:, 0]
    e_logits = jnp.einsum('td,gde->tge', xt, w_re).astype(jnp.float32) + b_re.astype(jnp.float32)
    e_logits = jnp.take_along_axis(e_logits, g_idx[:, None, None], axis=1)[:, 0]
    e_val, e_idx = lax.top_k(jax.nn.softmax(e_logits, axis=-1), TOPK_INNER)
    e_val = e_val / jnp.sum(e_val, axis=-1, keepdims=True)
    weights = g_val[:, None] * e_val
    eid = g_idx[:, None] * EXPERTS_PER_GROUP + e_idx
    N = T * TOPK_INNER
    M = DISPATCH_BLOCK
    flat_e = eid.reshape(N)
    flat_w = weights.reshape(N)
    flat_t = jnp.repeat(jnp.arange(T, dtype=jnp.int32), TOPK_INNER)
    order = jnp.argsort(flat_e)
    se, st, sw = flat_e[order], flat_t[order], flat_w[order]
    counts = jnp.bincount(flat_e, length=N_EXPERTS)
    starts = jnp.cumsum(counts) - counts
    padded = (counts + M - 1) // M * M
    pend = jnp.cumsum(padded)
    pstarts = pend - padded
    dest = pstarts[se] + (jnp.arange(N) - starts[se])
    P = N + N_EXPERTS * M
    nblk = P // M
    tok_buf = jnp.zeros((P,), jnp.int32).at[dest].set(st)
    w_buf = jnp.zeros((P,), jnp.float32).at[dest].set(sw)
    blk_e = jnp.clip(jnp.searchsorted(pend, jnp.arange(nblk) * M, side='right'), 0, N_EXPERTS - 1)

    def run(args):
        toks, wts, e = args
        xb = xt[toks]
        hid = jax.nn.silu(xb @ w_gate[e]) * (xb @ w_up[e])
        return (hid @ w_down[e]) * wts[:, None].astype(xb.dtype)

    yb = lax.map(run, (tok_buf.reshape(nblk, M), w_buf.reshape(nblk, M), blk_e))
    y = jax.ops.segment_sum(yb.reshape(P, D), tok_buf, num_segments=T)
    return y.reshape(B, S, D)


def setup_inputs(seed: int = 0) -> dict:
    key = jax.random.key(seed)
    ks = jax.random.split(key, 24)
    f32 = jnp.float32
    nrm = lambda k, shape, s: jax.random.normal(k, shape, f32) * s
    gain = lambda k, shape: 1.0 + 0.01 * jax.random.normal(k, shape, f32)
    L = DEPTH
    return {
        'x': nrm(ks[0], (BATCH, SEQ, D_MODEL), 1.0),
        'p': nrm(ks[1], (DEPTH, BATCH, SEQ, PLE_DIM), 1.0),
        'g_mix': gain(ks[2], (L, D_MODEL)),
        'w_in': nrm(ks[3], (L, D_MODEL, IN_WIDTH), D_MODEL ** -0.5),
        'beta_attn': gain(ks[4], (L, ATTN_WIDTH)),
        'w_pool': nrm(ks[5], (L, N_POOL_GROUPS, POOL_GROUP, POOL_GROUP), POOL_GROUP ** -0.5),
        'pool_scale': gain(ks[6], (L, POOL_WIDTH)),
        'w_out': nrm(ks[7], (L, MIX_WIDTH, D_MODEL), MIX_WIDTH ** -0.5),
        'g_ffn': gain(ks[8], (L, D_MODEL)),
        'w_router_group': nrm(ks[9], (L, D_MODEL, N_GROUPS), D_MODEL ** -0.5),
        'b_router_group': nrm(ks[10], (L, N_GROUPS), 0.01),
        'w_router_expert': nrm(ks[11], (L, N_GROUPS, D_MODEL, EXPERTS_PER_GROUP), D_MODEL ** -0.5),
        'b_router_expert': nrm(ks[12], (L, N_GROUPS, EXPERTS_PER_GROUP), 0.01),
        'w_expert_gate': nrm(ks[13], (L, N_EXPERTS, D_MODEL, D_EXPERT), D_MODEL ** -0.5),
        'w_expert_up': nrm(ks[14], (L, N_EXPERTS, D_MODEL, D_EXPERT), D_MODEL ** -0.5),
        'w_expert_down': nrm(ks[15], (L, N_EXPERTS, D_EXPERT, D_MODEL), D_EXPERT ** -0.5),
        'g_ple': gain(ks[16], (L, D_MODEL)),
        'w_ple': nrm(ks[17], (L, PLE_DIM, D_MODEL), PLE_DIM ** -0.5),
        'w_ple_gate': nrm(ks[18], (L, D_MODEL, D_MODEL), D_MODEL ** -0.5),
        'b_ple_gate': nrm(ks[19], (L, D_MODEL), 0.01),
        'g_final': gain(ks[20], (D_MODEL,)),
    }


def reference(x, p, g_mix, w_in, beta_attn, w_pool, pool_scale, w_out, g_ffn,
              w_router_group, b_router_group, w_router_expert, b_router_expert,
              w_expert_gate, w_expert_up, w_expert_down, g_ple, w_ple, w_ple_gate,
              b_ple_gate, g_final):
    B, S, _ = x.shape
    A = ATTN_WIDTH
    h = x
    for i in range(DEPTH):
        a = rms_norm(h, g_mix[i])
        proj = a @ w_in[i]
        q = proj[..., :A].reshape(B, S, N_HEADS, HEAD_DIM)
        k = proj[..., A:2 * A].reshape(B, S, N_HEADS, HEAD_DIM)
        v = proj[..., 2 * A:3 * A].reshape(B, S, N_HEADS, HEAD_DIM)
        u = proj[..., 3 * A:]
        o_attn = moba_attention(q, k, v)
        o_pool = pool_mixer(u, w_pool[i])
        mixed = jnp.concatenate([rms_norm(o_attn, beta_attn[i]),
                                 rms_norm(o_pool, pool_scale[i])], axis=-1)
        h = h + mixed @ w_out[i]
        f = rms_norm(h, g_ffn[i])
        h = h + hier_moe(f, w_router_group[i], b_router_group[i], w_router_expert[i],
                         b_router_expert[i], w_expert_gate[i], w_expert_up[i], w_expert_down[i])
        gate = jax.nn.sigmoid(rms_norm(h, g_ple[i]) @ w_ple_gate[i] + b_ple_gate[i])
        h = h + gate * (p[i] @ w_ple[i])
    return rms_norm(h, g_final)
```

```python
import functools

import jax
import jax.numpy as jnp
from jax import lax
from jax.experimental import pallas as pl
from jax.experimental.pallas import tpu as pltpu

F32 = jnp.float32
BF16 = jnp.bfloat16

N_HEADS = 8
HEAD_DIM = 128
MOBA_BLOCK = 256
MOBA_TOPK = 3
POOL_WINDOWS = (2, 4, 8, 16)
POOL_GROUP = 256
N_GROUPS = 8
EXPERTS_PER_GROUP = 8
N_EXPERTS = N_GROUPS * EXPERTS_PER_GROUP
EPS = 1e-6

LANES = 128
NEG = -1e30
VMEM_LIMIT = 56 * 1024 * 1024

INPROJ_TM = 512
MIX_TM = 256
POOL_HALO = 128
EXPERT_MB = 256
PLE_TM = 256

_NT = (((1,), (1,)), ((), ()))


def _rms(x, gain):
    return x * lax.rsqrt(jnp.mean(x * x, axis=-1, keepdims=True) + EPS) * gain


def _inproj_kernel(x_ref, g_ref, w_ref, proj_ref, kmean_ref, a_ref, *, tm, scale):
    j = pl.program_id(1)

    @pl.when(j == 0)
    def _():
        a_ref[...] = _rms(x_ref[...], g_ref[...]).astype(BF16)

    acc = jnp.dot(a_ref[...], w_ref[...], preferred_element_type=F32)
    proj_ref[...] = (acc * jnp.where(j == 0, scale, 1.0)).astype(BF16)

    @pl.when(j == 1)
    def _():
        kmean_ref[0] = jnp.mean(acc.reshape(tm // MOBA_BLOCK, MOBA_BLOCK, acc.shape[-1]), axis=1)


def _inproj(x2, g_mix, w_in_bf16, scale):
    T, D = x2.shape
    A = N_HEADS * HEAD_DIM
    tm = INPROJ_TM
    n_chunks = w_in_bf16.shape[1] // A
    return pl.pallas_call(
        functools.partial(_inproj_kernel, tm=tm, scale=scale),
        grid=(T // tm, n_chunks),
        in_specs=[
            pl.BlockSpec((tm, D), lambda i, j: (i, 0)),
            pl.BlockSpec((1, D), lambda i, j: (0, 0)),
            pl.BlockSpec((D, A), lambda i, j: (0, j)),
        ],
        out_specs=[
            pl.BlockSpec((tm, A), lambda i, j: (i, j)),
            pl.BlockSpec((1, tm // MOBA_BLOCK, A), lambda i, j: (i, 0, 0)),
        ],
        out_shape=[
            jax.ShapeDtypeStruct((T, n_chunks * A), BF16),
            jax.ShapeDtypeStruct((T // tm, tm // MOBA_BLOCK, A), F32),
        ],
        scratch_shapes=[pltpu.VMEM((tm, D), BF16)],
        compiler_params=pltpu.CompilerParams(
            dimension_semantics=("arbitrary", "arbitrary"), vmem_limit_bytes=VMEM_LIMIT),
        name="inproj",
    )(x2, g_mix.reshape(1, D), w_in_bf16)


def _attn_kernel(slopes_ref, q_ref, k_ref, v_ref, km_ref, o_ref,
                 kaug_ref, qaug_ref, m_ref, l_ref, acc_ref, *, nb):
    h = pl.program_id(1)
    i = pl.program_id(2)
    slope = slopes_ref[h]
    blk_rows = MOBA_BLOCK
    n_feat = LANES

    @pl.when(i == 0)
    def _build_kaug():
        def body(j, c):
            r0 = pl.multiple_of(j * blk_rows, blk_rows)
            kaug_ref[pl.ds(r0, blk_rows), 0:HEAD_DIM] = k_ref[pl.ds(r0, blk_rows), :]
            lane = lax.broadcasted_iota(jnp.int32, (blk_rows, n_feat), 1)
            kj = lax.broadcasted_iota(jnp.int32, (blk_rows, n_feat), 0).astype(F32)
            feat = jnp.where(lane == j, 1.0,
                             jnp.where(lane == nb, 1.0,
                                       jnp.where(lane == nb + 1, slope * kj, 0.0)))
            kaug_ref[pl.ds(r0, blk_rows), HEAD_DIM:HEAD_DIM + n_feat] = feat.astype(BF16)
            return c
        lax.fori_loop(0, nb, body, 0)

    q = q_ref[...]
    km = km_ref[...]
    km_hi = km.astype(BF16)
    km_lo = (km - km_hi.astype(F32)).astype(BF16)
    gT = (lax.dot_general(km_hi, q, _NT, preferred_element_type=F32)
          + lax.dot_general(km_lo, q, _NT, preferred_element_type=F32))
    blk = lax.broadcasted_iota(jnp.int32, (nb, blk_rows), 0)
    g = jnp.where(blk < i, gT, -jnp.inf)
    sel = blk == i
    for r in range(min(MOBA_TOPK, nb)):
        mx = jnp.max(g, axis=0, keepdims=True)
        idx = jnp.min(jnp.where(g == mx, blk, nb), axis=0, keepdims=True)
        pick = jnp.logical_and(blk == idx, r < i)
        sel = jnp.logical_or(sel, pick)
        g = jnp.where(pick, -jnp.inf, g)
    bias = jnp.where(sel, (-slope * blk_rows) * (i - blk).astype(F32), NEG)
    row2 = lax.broadcasted_iota(jnp.int32, (n_feat - nb, blk_rows), 0)
    qi = lax.broadcasted_iota(jnp.int32, (n_feat - nb, blk_rows), 1).astype(F32)
    extra = jnp.where(row2 == 0, -slope * qi, jnp.where(row2 == 1, 1.0, 0.0))
    feat_t = jnp.concatenate([bias, extra], axis=0)
    qaug_ref[:, 0:HEAD_DIM] = q
    qaug_ref[:, HEAD_DIM:HEAD_DIM + n_feat] = feat_t.T.astype(BF16)

    r_own = pl.multiple_of(i * blk_rows, blk_rows)
    s = lax.dot_general(qaug_ref[...], kaug_ref[pl.ds(r_own, blk_rows), :], _NT,
                        preferred_element_type=F32)
    row = lax.broadcasted_iota(jnp.int32, s.shape, 0)
    col = lax.broadcasted_iota(jnp.int32, s.shape, 1)
    s = jnp.where(col <= row, s, NEG)
    m0 = jnp.max(s, axis=1, keepdims=True)
    p = jnp.exp(s - m0)
    m_ref[...] = m0
    l_ref[...] = jnp.sum(p, axis=1, keepdims=True)
    acc_ref[...] = jnp.dot(p.astype(BF16), v_ref[pl.ds(r_own, blk_rows), :],
                           preferred_element_type=F32)

    def body(j, c):
        r0 = pl.multiple_of(j * blk_rows, blk_rows)
        s = lax.dot_general(qaug_ref[...], kaug_ref[pl.ds(r0, blk_rows), :], _NT,
                            preferred_element_type=F32)
        m_prev = m_ref[...]
        m_new = jnp.maximum(m_prev, jnp.max(s, axis=1, keepdims=True))
        alpha = jnp.exp(m_prev - m_new)
        p = jnp.exp(s - m_new)
        l_ref[...] = alpha * l_ref[...] + jnp.sum(p, axis=1, keepdims=True)
        acc_ref[...] = alpha * acc_ref[...] + jnp.dot(
            p.astype(BF16), v_ref[pl.ds(r0, blk_rows), :], preferred_element_type=F32)
        m_ref[...] = m_new
        return c
    lax.fori_loop(0, i, body, 0)

    o_ref[...] = (acc_ref[...] / l_ref[...]).astype(o_ref.dtype)


def _attention(proj3, kmean3, slopes):
    B, S, _ = proj3.shape
    nb = S // MOBA_BLOCK
    assert S % MOBA_BLOCK == 0 and nb % 8 == 0 and nb + 2 <= LANES
    A = N_HEADS * HEAD_DIM
    grid_spec = pltpu.PrefetchScalarGridSpec(
        num_scalar_prefetch=1,
        grid=(B, N_HEADS, nb),
        in_specs=[
            pl.BlockSpec((None, MOBA_BLOCK, HEAD_DIM), lambda b, h, i, s: (b, i, h)),
            pl.BlockSpec((None, S, HEAD_DIM), lambda b, h, i, s: (b, 0, N_HEADS + h)),
            pl.BlockSpec((None, S, HEAD_DIM), lambda b, h, i, s: (b, 0, 2 * N_HEADS + h)),
            pl.BlockSpec((None, nb, HEAD_DIM), lambda b, h, i, s: (b, 0, h)),
        ],
        out_specs=pl.BlockSpec((None, MOBA_BLOCK, HEAD_DIM), lambda b, h, i, s: (b, i, h)),
        scratch_shapes=[
            pltpu.VMEM((S, HEAD_DIM + LANES), BF16),
            pltpu.VMEM((MOBA_BLOCK, HEAD_DIM + LANES), BF16),
            pltpu.VMEM((MOBA_BLOCK, 1), F32),
            pltpu.VMEM((MOBA_BLOCK, 1), F32),
            pltpu.VMEM((MOBA_BLOCK, HEAD_DIM), F32),
        ],
    )
    return pl.pallas_call(
        functools.partial(_attn_kernel, nb=nb),
        grid_spec=grid_spec,
        out_shape=jax.ShapeDtypeStruct((B, S, A), BF16),
        compiler_params=pltpu.CompilerParams(
            dimension_semantics=("arbitrary", "arbitrary", "arbitrary"),
            vmem_limit_bytes=VMEM_LIMIT),
        name="moba_attn",
    )(slopes, proj3, proj3, proj3, kmean3)


def _mix_kernel(o_ref, u_ref, halo_ref, band_ref, wpool_ref, beta_ref, pscale_ref, wout_ref,
                x_ref, gffn_ref, wr_hi_ref, wr_lo_ref, br_ref,
                h1_ref, f_ref, eid_ref, wts_ref, *, tm, seq):
    i = pl.program_id(0)
    t0 = (i * tm) % seq
    u = u_ref[...]
    halo = jnp.where(t0 == 0, jnp.zeros_like(halo_ref[...]), halo_ref[...])
    uext = jnp.concatenate([halo, u], axis=0)
    t = t0 + lax.broadcasted_iota(jnp.int32, (tm, 1), 0)
    zs = []
    for g, w in enumerate(POOL_WINDOWS):
        cs = slice(g * POOL_GROUP, (g + 1) * POOL_GROUP)
        wsum = jnp.dot(band_ref[g], uext[:, cs], preferred_element_type=F32)
        cnt = jnp.minimum(t + 1, w).astype(F32)
        z = wsum / cnt - u[:, cs].astype(F32)
        zs.append(jnp.dot(z.astype(BF16), wpool_ref[g], preferred_element_type=F32))
    o_pool = jnp.concatenate(zs, axis=1)
    mixed = jnp.concatenate(
        [_rms(o_ref[...].astype(F32), beta_ref[...]).astype(BF16),
         _rms(o_pool, pscale_ref[...]).astype(BF16)], axis=1)
    h1 = x_ref[...] + jnp.dot(mixed, wout_ref[...], preferred_element_type=F32)
    h1_ref[...] = h1
    f = _rms(h1, gffn_ref[...])
    f_ref[...] = f

    f_hi = f.astype(BF16)
    f_lo = (f - f_hi.astype(F32)).astype(BF16)
    wr_hi = wr_hi_ref[...]
    logits = (jnp.dot(f_hi, wr_hi, preferred_element_type=F32)
              + jnp.dot(f_lo, wr_hi, preferred_element_type=F32)
              + jnp.dot(f_hi, wr_lo_ref[...], preferred_element_type=F32))
    lt = logits.T + br_ref[...]
    row8 = lax.broadcasted_iota(jnp.int32, (N_GROUPS, tm), 0)
    gl = lt[0:N_GROUPS]
    gmax = jnp.max(gl, axis=0, keepdims=True)
    gidx = jnp.min(jnp.where(gl == gmax, row8, N_GROUPS), axis=0, keepdims=True)
    g_val = 1.0 / jnp.sum(jnp.exp(gl - gmax), axis=0, keepdims=True)
    el = jnp.zeros((EXPERTS_PER_GROUP, tm), F32)
    for g in range(N_GROUPS):
        lo = N_GROUPS + g * EXPERTS_PER_GROUP
        el = jnp.where(gidx == g, lt[lo:lo + EXPERTS_PER_GROUP], el)
    emax = jnp.max(el, axis=0, keepdims=True)
    esum = jnp.sum(jnp.exp(el - emax), axis=0, keepdims=True)
    i1 = jnp.min(jnp.where(el == emax, row8, EXPERTS_PER_GROUP), axis=0, keepdims=True)
    el2 = jnp.where(row8 == i1, -jnp.inf, el)
    m2 = jnp.max(el2, axis=0, keepdims=True)
    i2 = jnp.min(jnp.where(el2 == m2, row8, EXPERTS_PER_GROUP), axis=0, keepdims=True)
    p1 = 1.0 / esum
    p2 = jnp.exp(m2 - emax) / esum
    den = p1 + p2
    eid_ref[0:1, :] = gidx * EXPERTS_PER_GROUP + i1
    eid_ref[1:2, :] = gidx * EXPERTS_PER_GROUP + i2
    wts_ref[0:1, :] = g_val * (p1 / den)
    wts_ref[1:2, :] = g_val * (p2 / den)


def _pool_band(tm):
    r = jnp.arange(tm)[:, None]
    c = jnp.arange(POOL_HALO + tm)[None, :] - POOL_HALO
    return jnp.stack([((c <= r) & (c >= r - w + 1)).astype(BF16) for w in POOL_WINDOWS])


def _mix(o2, proj, x2, seq, w_pool, beta_attn, pool_scale, w_out, g_ffn, w_rg, b_rg, w_re, b_re):
    T, D = x2.shape
    A = N_HEADS * HEAD_DIM
    PW = len(POOL_WINDOWS) * POOL_GROUP
    tm = MIX_TM
    assert seq % tm == 0 and tm % POOL_HALO == 0 and POOL_HALO >= max(POOL_WINDOWS) - 1
    u_col = (proj.shape[1] - PW) // PW
    halo_per_tile = tm // POOL_HALO
    wr = jnp.concatenate([w_rg, jnp.transpose(w_re, (1, 0, 2)).reshape(D, N_EXPERTS)], axis=1)
    wr = jnp.pad(wr, ((0, 0), (0, LANES - wr.shape[1])))
    wr_hi = wr.astype(BF16)
    wr_lo = (wr - wr_hi.astype(F32)).astype(BF16)
    br = jnp.pad(jnp.concatenate([b_rg, b_re.reshape(-1)]), (0, LANES - N_GROUPS - N_EXPERTS))
    const = lambda shape: pl.BlockSpec(shape, lambda i: (0,) * len(shape))
    return pl.pallas_call(
        functools.partial(_mix_kernel, tm=tm, seq=seq),
        grid=(T // tm,),
        in_specs=[
            pl.BlockSpec((tm, A), lambda i: (i, 0)),
            pl.BlockSpec((tm, PW), lambda i: (i, u_col)),
            pl.BlockSpec((POOL_HALO, PW),
                         lambda i: (jnp.maximum(i * halo_per_tile - 1, 0), u_col)),
            const((len(POOL_WINDOWS), tm, POOL_HALO + tm)),
            const((len(POOL_WINDOWS), POOL_GROUP, POOL_GROUP)),
            const((1, A)), const((1, PW)), const((A + PW, D)),
            pl.BlockSpec((tm, D), lambda i: (i, 0)),
            const((1, D)), const((D, LANES)), const((D, LANES)), const((LANES, 1)),
        ],
        out_specs=[
            pl.BlockSpec((tm, D), lambda i: (i, 0)),
            pl.BlockSpec((tm, D), lambda i: (i, 0)),
            pl.BlockSpec((2, tm), lambda i: (0, i)),
            pl.BlockSpec((2, tm), lambda i: (0, i)),
        ],
        out_shape=[
            jax.ShapeDtypeStruct((T, D), F32),
            jax.ShapeDtypeStruct((T, D), F32),
            jax.ShapeDtypeStruct((2, T), jnp.int32),
            jax.ShapeDtypeStruct((2, T), F32),
        ],
        compiler_params=pltpu.CompilerParams(
            dimension_semantics=("arbitrary",), vmem_limit_bytes=VMEM_LIMIT),
        name="mix_router",
    )(o2, proj, proj, _pool_band(tm), w_pool.astype(BF16), beta_attn.reshape(1, A),
      pool_scale.reshape(1, PW), w_out.astype(BF16), x2, g_ffn.reshape(1, D),
      wr_hi, wr_lo, br.reshape(LANES, 1))


def _expert_kernel(blk_e_ref, n_used_ref, tok_ref, tok_next_ref, slot_ref, w_ref,
                   wg_ref, wu_ref, wd_ref, f_hbm, y_hbm,
                   xbuf, ybuf, wg_s, wu_s, wd_s, gsem, ssem, *, mb, n_rows):
    b = pl.program_id(0)
    n_used = n_used_ref[0]
    cur = b % 2

    def gather_copy(tok, r, slot):
        return pltpu.make_async_copy(f_hbm.at[pl.ds(tok, 1), :],
                                     xbuf.at[slot, pl.ds(r, 1), :], gsem.at[slot])

    def scatter_copy(dst, r, slot):
        return pltpu.make_async_copy(ybuf.at[slot, pl.ds(r, 1), :],
                                     y_hbm.at[pl.ds(dst, 1), :], ssem.at[slot])

    def start_gather(idx_ref, slot):
        def body(r, c):
            gather_copy(idx_ref[0, 0, r], r, slot).start()
            return c
        lax.fori_loop(0, mb, body, 0, unroll=8)

    def wait_rows(copy_fn, slot):
        def body(r, c):
            copy_fn(0, r, slot).wait()
            return c
        lax.fori_loop(0, mb, body, 0, unroll=8)

    @pl.when(b < n_used)
    def _active():
        @pl.when(b == 0)
        def _():
            start_gather(tok_ref, 0)
            ybuf[1] = jnp.zeros(ybuf.shape[1:], ybuf.dtype)
            for half in range(2):
                fill = pltpu.make_async_copy(
                    ybuf.at[1], y_hbm.at[pl.ds(n_rows + half * mb, mb), :], ssem.at[1])
                fill.start()
                fill.wait()

        @pl.when(b + 1 < n_used)
        def _():
            start_gather(tok_next_ref, 1 - cur)

        first_of_expert = jnp.logical_or(b == 0, blk_e_ref[b] != blk_e_ref[jnp.maximum(b - 1, 0)])

        @pl.when(first_of_expert)
        def _():
            wg_s[...] = wg_ref[...].astype(BF16)
            wu_s[...] = wu_ref[...].astype(BF16)
            wd_s[...] = wd_ref[...].astype(BF16)

        wait_rows(gather_copy, cur)

        @pl.when(b >= 2)
        def _():
            wait_rows(scatter_copy, cur)

        xb = xbuf[cur].astype(BF16)
        gate = jnp.dot(xb, wg_s[...], preferred_element_type=F32)
        up = jnp.dot(xb, wu_s[...], preferred_element_type=F32)
        hid = gate * jax.nn.sigmoid(gate) * up
        y = jnp.dot(hid.astype(BF16), wd_s[...], preferred_element_type=F32)
        ybuf[cur] = y * w_ref[...]

        def body(r, c):
            scatter_copy(slot_ref[0, 0, r], r, cur).start()
            return c
        lax.fori_loop(0, mb, body, 0, unroll=8)

        @pl.when(b == n_used - 1)
        def _drain():
            @pl.when(b >= 1)
            def _():
                wait_rows(scatter_copy, 1 - cur)
            wait_rows(scatter_copy, cur)


def _experts(f2, eid, wts, w_gate, w_up, w_down):
    T, D = f2.shape
    DE = w_gate.shape[-1]
    mb = EXPERT_MB
    N = 2 * T
    flat_e = eid.T.reshape(N)
    flat_w = wts.T.reshape(N)
    order = jnp.argsort(flat_e, stable=True).astype(jnp.int32)
    se = flat_e[order]
    counts = jnp.bincount(flat_e, length=N_EXPERTS).astype(jnp.int32)
    starts = jnp.cumsum(counts) - counts
    padded = (counts + mb - 1) // mb * mb
    pend = jnp.cumsum(padded)
    pstarts = pend - padded
    dest = pstarts[se] + (jnp.arange(N, dtype=jnp.int32) - starts[se])
    P = N + N_EXPERTS * mb
    nblk = P // mb
    tok_buf = jnp.zeros((P,), jnp.int32).at[dest].set(order // 2)
    slot_row = (order % 2) * T + order // 2
    pos = jnp.arange(P, dtype=jnp.int32)
    spare_row = N + ((pos // mb) % 2) * mb + pos % mb
    slot_buf = spare_row.at[dest].set(slot_row)
    w_buf = jnp.zeros((P,), F32).at[dest].set(flat_w[order])
    n_used = (pend[-1] // mb).astype(jnp.int32)
    blk_e = jnp.searchsorted(pend, jnp.arange(nblk, dtype=jnp.int32) * mb, side="right")
    last_e = jnp.searchsorted(pend, (n_used - 1) * mb, side="right")
    blk_e = jnp.minimum(blk_e, last_e).astype(jnp.int32)

    grid_spec = pltpu.PrefetchScalarGridSpec(
        num_scalar_prefetch=2,
        grid=(nblk,),
        in_specs=[
            pl.BlockSpec((1, 1, mb), lambda b, be, nu: (b, 0, 0), memory_space=pltpu.SMEM),
            pl.BlockSpec((1, 1, mb), lambda b, be, nu: (jnp.minimum(b + 1, nblk - 1), 0, 0),
                         memory_space=pltpu.SMEM),
            pl.BlockSpec((1, 1, mb), lambda b, be, nu: (b, 0, 0), memory_space=pltpu.SMEM),
            pl.BlockSpec((mb, 1), lambda b, be, nu: (b, 0)),
            pl.BlockSpec((None, D, DE), lambda b, be, nu: (be[b], 0, 0)),
            pl.BlockSpec((None, D, DE), lambda b, be, nu: (be[b], 0, 0)),
            pl.BlockSpec((None, DE, D), lambda b, be, nu: (be[b], 0, 0)),
            pl.BlockSpec(memory_space=pl.ANY),
        ],
        out_specs=pl.BlockSpec(memory_space=pl.ANY),
        scratch_shapes=[
            pltpu.VMEM((2, mb, D), F32),
            pltpu.VMEM((2, mb, D), F32),
            pltpu.VMEM((D, DE), BF16), pltpu.VMEM((D, DE), BF16), pltpu.VMEM((DE, D), BF16),
            pltpu.SemaphoreType.DMA((2,)), pltpu.SemaphoreType.DMA((2,)),
        ],
    )
    y = pl.pallas_call(
        functools.partial(_expert_kernel, mb=mb, n_rows=N),
        grid_spec=grid_spec,
        out_shape=jax.ShapeDtypeStruct((N + 2 * mb, D), F32),
        compiler_params=pltpu.CompilerParams(
            dimension_semantics=("arbitrary",), vmem_limit_bytes=VMEM_LIMIT),
        name="experts",
    )(blk_e, n_used.reshape(1), tok_buf.reshape(nblk, 1, mb), tok_buf.reshape(nblk, 1, mb),
      slot_buf.reshape(nblk, 1, mb), w_buf.reshape(P, 1), w_gate, w_up, w_down, f2)
    return y


def _ple_kernel(h1_ref, y0_ref, y1_ref, p_ref, gple_ref, wg_ref, bg_ref, wple_ref, gfin_ref, out_ref,
                *, final):
    h = h1_ref[...] + y0_ref[...] + y1_ref[...]
    a = _rms(h, gple_ref[...]).astype(BF16)
    gate = jax.nn.sigmoid(jnp.dot(a, wg_ref[...], preferred_element_type=F32) + bg_ref[...])
    ple = jnp.dot(p_ref[...].astype(BF16), wple_ref[...], preferred_element_type=F32)
    h = h + gate * ple
    out_ref[...] = _rms(h, gfin_ref[...]) if final else h


def _ple(h1, y3, p2, g_ple, w_ple_gate, b_ple_gate, w_ple, g_final, final):
    T, D = h1.shape
    PD = p2.shape[1]
    tm = PLE_TM
    const = lambda shape: pl.BlockSpec(shape, lambda i: (0,) * len(shape))
    return pl.pallas_call(
        functools.partial(_ple_kernel, final=final),
        grid=(T // tm,),
        in_specs=[
            pl.BlockSpec((tm, D), lambda i: (i, 0)),
            pl.BlockSpec((tm, D), lambda i: (i, 0)),
            pl.BlockSpec((tm, D), lambda i: (T // tm + i, 0)),
            pl.BlockSpec((tm, PD), lambda i: (i, 0)),
            const((1, D)), const((D, D)), const((1, D)), const((PD, D)), const((1, D)),
        ],
        out_specs=pl.BlockSpec((tm, D), lambda i: (i, 0)),
        out_shape=jax.ShapeDtypeStruct((T, D), F32),
        compiler_params=pltpu.CompilerParams(
            dimension_semantics=("arbitrary",), vmem_limit_bytes=VMEM_LIMIT),
        name="ple_final",
    )(h1, y3, y3, p2, g_ple.reshape(1, D), w_ple_gate.astype(BF16), b_ple_gate.reshape(1, D),
      w_ple.astype(BF16), g_final.reshape(1, D))


def kernel(x, p, g_mix, w_in, beta_attn, w_pool, pool_scale, w_out, g_ffn, w_router_group,
           b_router_group, w_router_expert, b_router_expert, w_expert_gate, w_expert_up,
           w_expert_down, g_ple, w_ple, w_ple_gate, b_ple_gate, g_final):
    B, S, D = x.shape
    depth = w_in.shape[0]
    T = B * S
    A = N_HEADS * HEAD_DIM
    slopes = jnp.asarray([2.0 ** (-8.0 * (h + 1) / N_HEADS) for h in range(N_HEADS)], F32)
    h = x.reshape(T, D)
    for i in range(depth):
        proj, kmean = _inproj(h, g_mix[i], w_in[i].astype(BF16), HEAD_DIM ** -0.5)
        o = _attention(proj.reshape(B, S, -1), kmean.reshape(B, S // MOBA_BLOCK, A), slopes)
        h1, f, eid, wts = _mix(o.reshape(T, A), proj, h, S, w_pool[i], beta_attn[i], pool_scale[i],
                               w_out[i], g_ffn[i], w_router_group[i], b_router_group[i],
                               w_router_expert[i], b_router_expert[i])
        y3 = _experts(f, eid, wts, w_expert_gate[i], w_expert_up[i], w_expert_down[i])
        h = _ple(h1, y3, p[i].reshape(T, -1), g_ple[i], w_ple_gate[i], b_ple_gate[i], w_ple[i],
                 g_final, final=(i == depth - 1))
    return h.reshape(B, S, D)
```

```python
import functools

import jax
import jax.numpy as jnp
from jax import lax
from jax.experimental import pallas as pl
from jax.experimental.pallas import tpu as pltpu

F32 = jnp.float32
BF16 = jnp.bfloat16

N_HEADS = 8
HEAD_DIM = 128
MOBA_BLOCK = 256
MOBA_TOPK = 3
POOL_WINDOWS = (2, 4, 8, 16)
POOL_GROUP = 256
N_GROUPS = 8
EXPERTS_PER_GROUP = 8
N_EXPERTS = N_GROUPS * EXPERTS_PER_GROUP
EPS = 1e-6

LANES = 128
NEG = -1e30
VMEM_LIMIT = 56 * 1024 * 1024

INPROJ_TM = 512
MIX_TM = 256
POOL_HALO = 128
EXPERT_MB = 256
PLE_TM = 256
ATTN_HEADS_PER_STEP = 2
V_PAD_ROWS = 16

_NT = (((1,), (1,)), ((), ()))


def _rms(x, gain):
    return x * lax.rsqrt(jnp.mean(x * x, axis=-1, keepdims=True) + EPS) * gain


def _inproj_kernel(x_ref, g_ref, w_ref, proj_ref, kmean_ref, a_ref, *, tm, scale):
    j = pl.program_id(1)

    @pl.when(j == 0)
    def _():
        a_ref[...] = _rms(x_ref[...], g_ref[...]).astype(BF16)

    acc = jnp.dot(a_ref[...], w_ref[...], preferred_element_type=F32)
    proj_ref[...] = (acc * jnp.where(j == 0, scale, 1.0)).astype(BF16)

    @pl.when(j == 1)
    def _():
        kmean_ref[0] = jnp.mean(acc.reshape(tm // MOBA_BLOCK, MOBA_BLOCK, acc.shape[-1]), axis=1)


def _inproj(x2, g_mix, w_in_bf16, scale):
    T, D = x2.shape
    A = N_HEADS * HEAD_DIM
    tm = INPROJ_TM
    n_chunks = w_in_bf16.shape[1] // A
    return pl.pallas_call(
        functools.partial(_inproj_kernel, tm=tm, scale=scale),
        grid=(T // tm, n_chunks),
        in_specs=[
            pl.BlockSpec((tm, D), lambda i, j: (i, 0)),
            pl.BlockSpec((1, D), lambda i, j: (0, 0)),
            pl.BlockSpec((D, A), lambda i, j: (0, j)),
        ],
        out_specs=[
            pl.BlockSpec((tm, A), lambda i, j: (i, j)),
            pl.BlockSpec((1, tm // MOBA_BLOCK, A), lambda i, j: (i, 0, 0)),
        ],
        out_shape=[
            jax.ShapeDtypeStruct((T, n_chunks * A), BF16),
            jax.ShapeDtypeStruct((T // tm, tm // MOBA_BLOCK, A), F32),
        ],
        scratch_shapes=[pltpu.VMEM((tm, D), BF16)],
        compiler_params=pltpu.CompilerParams(
            dimension_semantics=("arbitrary", "arbitrary"), vmem_limit_bytes=VMEM_LIMIT),
        name="inproj",
    )(x2, g_mix.reshape(1, D), w_in_bf16)


def _attn_kernel(slopes_ref, q_ref, k_ref, v_ref, km_ref, o_ref,
                 kaug_ref, vt_ref, qaug_ref, s_ref, bmax_ref, m_ref, acc_ref, *, nb, hpb):
    hg = pl.program_id(1)
    i = pl.program_id(2)
    blk_rows = MOBA_BLOCK
    n_feat = LANES

    @pl.when(i == 0)
    def _build_kv():
        for hh in range(hpb):
            slope = slopes_ref[hg * hpb + hh]
            cs = slice(hh * HEAD_DIM, (hh + 1) * HEAD_DIM)

            def body(j, c, hh=hh, slope=slope, cs=cs):
                r0 = pl.multiple_of(j * blk_rows, blk_rows)
                kaug_ref[hh, j, :, 0:HEAD_DIM] = k_ref[pl.ds(r0, blk_rows), cs]
                lane = lax.broadcasted_iota(jnp.int32, (blk_rows, n_feat), 1)
                kj = lax.broadcasted_iota(jnp.int32, (blk_rows, n_feat), 0).astype(F32)
                feat = jnp.where(lane == j, 1.0,
                                 jnp.where(lane == nb, 1.0,
                                           jnp.where(lane == nb + 1, slope * kj, 0.0)))
                kaug_ref[hh, j, :, HEAD_DIM:HEAD_DIM + n_feat] = feat.astype(BF16)
                vt_ref[hh, j, 0:HEAD_DIM, :] = v_ref[pl.ds(r0, blk_rows), cs].astype(F32).T.astype(BF16)
                ones_row = lax.broadcasted_iota(jnp.int32, (V_PAD_ROWS, blk_rows), 0) == 0
                vt_ref[hh, j, HEAD_DIM:HEAD_DIM + V_PAD_ROWS, :] = jnp.where(ones_row, 1.0, 0.0).astype(BF16)
                return c
            lax.fori_loop(0, nb, body, 0)

    blk = lax.broadcasted_iota(jnp.int32, (nb, blk_rows), 0)
    row2 = lax.broadcasted_iota(jnp.int32, (n_feat - nb, blk_rows), 0)
    qi = lax.broadcasted_iota(jnp.int32, (n_feat - nb, blk_rows), 1).astype(F32)
    for hh in range(hpb):
        slope = slopes_ref[hg * hpb + hh]
        cs = slice(hh * HEAD_DIM, (hh + 1) * HEAD_DIM)
        q_t = q_ref[:, cs].astype(F32).T.astype(BF16)
        km = km_ref[:, cs]
        km_hi = km.astype(BF16)
        km_lo = (km - km_hi.astype(F32)).astype(BF16)
        g = (jnp.dot(km_hi, q_t, preferred_element_type=F32)
             + jnp.dot(km_lo, q_t, preferred_element_type=F32))
        g = jnp.where(blk < i, g, -jnp.inf)
        sel = blk == i
        for r in range(min(MOBA_TOPK, nb)):
            mx = jnp.max(g, axis=0, keepdims=True)
            idx = jnp.min(jnp.where(g == mx, blk, nb), axis=0, keepdims=True)
            pick = jnp.logical_and(blk == idx, r < i)
            sel = jnp.logical_or(sel, pick)
            g = jnp.where(pick, -jnp.inf, g)
        bias = jnp.where(sel, (-slope * blk_rows) * (i - blk).astype(F32), NEG)
        extra = jnp.where(row2 == 0, -slope * qi, jnp.where(row2 == 1, 1.0, 0.0))
        qaug_ref[hh, 0:HEAD_DIM, :] = q_t
        qaug_ref[hh, HEAD_DIM:HEAD_DIM + nb, :] = bias.astype(BF16)
        qaug_ref[hh, HEAD_DIM + nb:HEAD_DIM + n_feat, :] = extra.astype(BF16)

    def block_at(t):
        return jnp.where(t == 0, i, jnp.where(t <= i, t - 1, jnp.minimum(i + 1, nb - 1)))

    def scores(slot, b, causal):
        for hh in range(hpb):
            s = jnp.dot(kaug_ref[hh, b], qaug_ref[hh], preferred_element_type=F32)
            if causal:
                key_i = lax.broadcasted_iota(jnp.int32, (blk_rows, blk_rows), 0)
                qry_i = lax.broadcasted_iota(jnp.int32, (blk_rows, blk_rows), 1)
                s = jnp.where(key_i <= qry_i, s, NEG)
            s_ref[slot, hh] = s
            bmax_ref[slot, hh] = jnp.max(s, axis=0, keepdims=True)

    def accumulate(slot, b):
        for hh in range(hpb):
            m_prev = m_ref[hh]
            m_new = jnp.maximum(m_prev, bmax_ref[slot, hh])
            alpha = jnp.exp(m_prev - m_new)
            p = jnp.exp(s_ref[slot, hh] - m_new)
            acc_ref[hh] = alpha * acc_ref[hh] + jnp.dot(
                vt_ref[hh, b], p.astype(BF16), preferred_element_type=F32)
            m_ref[hh] = m_new

    m_ref[...] = jnp.full(m_ref.shape, -jnp.inf, F32)
    acc_ref[...] = jnp.zeros(acc_ref.shape, F32)
    scores(0, i, causal=True)

    def body(t, c):
        b1 = block_at(2 * t + 1)
        scores(1, b1, causal=False)
        accumulate(0, block_at(2 * t))
        scores(0, block_at(2 * t + 2), causal=False)
        accumulate(1, b1)
        return c
    lax.fori_loop(0, (i + 2) // 2, body, 0)

    for hh in range(hpb):
        o_t = acc_ref[hh, 0:HEAD_DIM, :] / acc_ref[hh, HEAD_DIM:HEAD_DIM + 1, :]
        o_ref[:, hh * HEAD_DIM:(hh + 1) * HEAD_DIM] = o_t.T.astype(o_ref.dtype)


def _attention(proj3, kmean3, slopes):
    B, S, _ = proj3.shape
    nb = S // MOBA_BLOCK
    hpb = ATTN_HEADS_PER_STEP
    assert S % MOBA_BLOCK == 0 and nb % 16 == 0 and nb + 2 <= LANES and N_HEADS % hpb == 0
    A = N_HEADS * HEAD_DIM
    w = hpb * HEAD_DIM
    n_hg = N_HEADS // hpb
    grid_spec = pltpu.PrefetchScalarGridSpec(
        num_scalar_prefetch=1,
        grid=(B, n_hg, nb),
        in_specs=[
            pl.BlockSpec((None, MOBA_BLOCK, w), lambda b, h, i, s: (b, i, h)),
            pl.BlockSpec((None, S, w), lambda b, h, i, s: (b, 0, n_hg + h)),
            pl.BlockSpec((None, S, w), lambda b, h, i, s: (b, 0, 2 * n_hg + h)),
            pl.BlockSpec((None, nb, w), lambda b, h, i, s: (b, 0, h)),
        ],
        out_specs=pl.BlockSpec((None, MOBA_BLOCK, w), lambda b, h, i, s: (b, i, h)),
        scratch_shapes=[
            pltpu.VMEM((hpb, nb, MOBA_BLOCK, HEAD_DIM + LANES), BF16),
            pltpu.VMEM((hpb, nb, HEAD_DIM + V_PAD_ROWS, MOBA_BLOCK), BF16),
            pltpu.VMEM((hpb, HEAD_DIM + LANES, MOBA_BLOCK), BF16),
            pltpu.VMEM((2, hpb, MOBA_BLOCK, MOBA_BLOCK), F32),
            pltpu.VMEM((2, hpb, 1, MOBA_BLOCK), F32),
            pltpu.VMEM((hpb, 1, MOBA_BLOCK), F32),
            pltpu.VMEM((hpb, HEAD_DIM + V_PAD_ROWS, MOBA_BLOCK), F32),
        ],
    )
    return pl.pallas_call(
        functools.partial(_attn_kernel, nb=nb, hpb=hpb),
        grid_spec=grid_spec,
        out_shape=jax.ShapeDtypeStruct((B, S, A), BF16),
        compiler_params=pltpu.CompilerParams(
            dimension_semantics=("arbitrary", "arbitrary", "arbitrary"),
            vmem_limit_bytes=VMEM_LIMIT),
        name="moba_attn",
    )(slopes, proj3, proj3, proj3, kmean3)


def _mix_kernel(o_ref, u_ref, halo_ref, band_ref, wpool_ref, beta_ref, pscale_ref, wout_ref,
                x_ref, gffn_ref, wr_hi_ref, wr_lo_ref, br_ref,
                h1_ref, f_ref, eid_ref, wts_ref, *, tm, seq):
    i = pl.program_id(0)
    t0 = (i * tm) % seq
    u = u_ref[...]
    halo = jnp.where(t0 == 0, jnp.zeros_like(halo_ref[...]), halo_ref[...])
    uext = jnp.concatenate([halo, u], axis=0)
    t = t0 + lax.broadcasted_iota(jnp.int32, (tm, 1), 0)
    zs = []
    for g, w in enumerate(POOL_WINDOWS):
        cs = slice(g * POOL_GROUP, (g + 1) * POOL_GROUP)
        wsum = jnp.dot(band_ref[g], uext[:, cs], preferred_element_type=F32)
        cnt = jnp.minimum(t + 1, w).astype(F32)
        z = wsum / cnt - u[:, cs].astype(F32)
        zs.append(jnp.dot(z.astype(BF16), wpool_ref[g], preferred_element_type=F32))
    o_pool = jnp.concatenate(zs, axis=1)
    mixed = jnp.concatenate(
        [_rms(o_ref[...].astype(F32), beta_ref[...]).astype(BF16),
         _rms(o_pool, pscale_ref[...]).astype(BF16)], axis=1)
    h1 = x_ref[...] + jnp.dot(mixed, wout_ref[...], preferred_element_type=F32)
    h1_ref[...] = h1
    f = _rms(h1, gffn_ref[...])
    f_ref[...] = f

    f_hi = f.astype(BF16)
    f_lo = (f - f_hi.astype(F32)).astype(BF16)
    wr_hi = wr_hi_ref[...]
    logits = (jnp.dot(f_hi, wr_hi, preferred_element_type=F32)
              + jnp.dot(f_lo, wr_hi, preferred_element_type=F32)
              + jnp.dot(f_hi, wr_lo_ref[...], preferred_element_type=F32))
    lt = logits.T + br_ref[...]
    row8 = lax.broadcasted_iota(jnp.int32, (N_GROUPS, tm), 0)
    gl = lt[0:N_GROUPS]
    gmax = jnp.max(gl, axis=0, keepdims=True)
    gidx = jnp.min(jnp.where(gl == gmax, row8, N_GROUPS), axis=0, keepdims=True)
    g_val = 1.0 / jnp.sum(jnp.exp(gl - gmax), axis=0, keepdims=True)
    el = jnp.zeros((EXPERTS_PER_GROUP, tm), F32)
    for g in range(N_GROUPS):
        lo = N_GROUPS + g * EXPERTS_PER_GROUP
        el = jnp.where(gidx == g, lt[lo:lo + EXPERTS_PER_GROUP], el)
    emax = jnp.max(el, axis=0, keepdims=True)
    esum = jnp.sum(jnp.exp(el - emax), axis=0, keepdims=True)
    i1 = jnp.min(jnp.where(el == emax, row8, EXPERTS_PER_GROUP), axis=0, keepdims=True)
    el2 = jnp.where(row8 == i1, -jnp.inf, el)
    m2 = jnp.max(el2, axis=0, keepdims=True)
    i2 = jnp.min(jnp.where(el2 == m2, row8, EXPERTS_PER_GROUP), axis=0, keepdims=True)
    p1 = 1.0 / esum
    p2 = jnp.exp(m2 - emax) / esum
    den = p1 + p2
    eid_ref[0:1, :] = gidx * EXPERTS_PER_GROUP + i1
    eid_ref[1:2, :] = gidx * EXPERTS_PER_GROUP + i2
    wts_ref[0:1, :] = g_val * (p1 / den)
    wts_ref[1:2, :] = g_val * (p2 / den)


def _pool_band(tm):
    r = jnp.arange(tm)[:, None]
    c = jnp.arange(POOL_HALO + tm)[None, :] - POOL_HALO
    return jnp.stack([((c <= r) & (c >= r - w + 1)).astype(BF16) for w in POOL_WINDOWS])


def _mix(o2, proj, x2, seq, w_pool, beta_attn, pool_scale, w_out, g_ffn, w_rg, b_rg, w_re, b_re):
    T, D = x2.shape
    A = N_HEADS * HEAD_DIM
    PW = len(POOL_WINDOWS) * POOL_GROUP
    tm = MIX_TM
    assert seq % tm == 0 and tm % POOL_HALO == 0 and POOL_HALO >= max(POOL_WINDOWS) - 1
    u_col = (proj.shape[1] - PW) // PW
    halo_per_tile = tm // POOL_HALO
    wr = jnp.concatenate([w_rg, jnp.transpose(w_re, (1, 0, 2)).reshape(D, N_EXPERTS)], axis=1)
    wr = jnp.pad(wr, ((0, 0), (0, LANES - wr.shape[1])))
    wr_hi = wr.astype(BF16)
    wr_lo = (wr - wr_hi.astype(F32)).astype(BF16)
    br = jnp.pad(jnp.concatenate([b_rg, b_re.reshape(-1)]), (0, LANES - N_GROUPS - N_EXPERTS))
    const = lambda shape: pl.BlockSpec(shape, lambda i: (0,) * len(shape))
    return pl.pallas_call(
        functools.partial(_mix_kernel, tm=tm, seq=seq),
        grid=(T // tm,),
        in_specs=[
            pl.BlockSpec((tm, A), lambda i: (i, 0)),
            pl.BlockSpec((tm, PW), lambda i: (i, u_col)),
            pl.BlockSpec((POOL_HALO, PW),
                         lambda i: (jnp.maximum(i * halo_per_tile - 1, 0), u_col)),
            const((len(POOL_WINDOWS), tm, POOL_HALO + tm)),
            const((len(POOL_WINDOWS), POOL_GROUP, POOL_GROUP)),
            const((1, A)), const((1, PW)), const((A + PW, D)),
            pl.BlockSpec((tm, D), lambda i: (i, 0)),
            const((1, D)), const((D, LANES)), const((D, LANES)), const((LANES, 1)),
        ],
        out_specs=[
            pl.BlockSpec((tm, D), lambda i: (i, 0)),
            pl.BlockSpec((tm, D), lambda i: (i, 0)),
            pl.BlockSpec((2, tm), lambda i: (0, i)),
            pl.BlockSpec((2, tm), lambda i: (0, i)),
        ],
        out_shape=[
            jax.ShapeDtypeStruct((T, D), F32),
            jax.ShapeDtypeStruct((T, D), F32),
            jax.ShapeDtypeStruct((2, T), jnp.int32),
            jax.ShapeDtypeStruct((2, T), F32),
        ],
        compiler_params=pltpu.CompilerParams(
            dimension_semantics=("arbitrary",), vmem_limit_bytes=VMEM_LIMIT),
        name="mix_router",
    )(o2, proj, proj, _pool_band(tm), w_pool.astype(BF16), beta_attn.reshape(1, A),
      pool_scale.reshape(1, PW), w_out.astype(BF16), x2, g_ffn.reshape(1, D),
      wr_hi, wr_lo, br.reshape(LANES, 1))


def _expert_kernel(blk_e_ref, n_used_ref, tok_ref, tok_next_ref, slot_ref, w_ref,
                   wg_ref, wu_ref, wd_ref, f_hbm, y_hbm,
                   xbuf, ybuf, wg_s, wu_s, wd_s, gsem, ssem, *, mb, n_rows):
    b = pl.program_id(0)
    n_used = n_used_ref[0]
    cur = b % 2

    def gather_copy(tok, r, slot):
        return pltpu.make_async_copy(f_hbm.at[pl.ds(tok, 1), :],
                                     xbuf.at[slot, pl.ds(r, 1), :], gsem.at[slot])

    def scatter_copy(dst, r, slot):
        return pltpu.make_async_copy(ybuf.at[slot, pl.ds(r, 1), :],
                                     y_hbm.at[pl.ds(dst, 1), :], ssem.at[slot])

    def start_gather(idx_ref, slot):
        def body(r, c):
            gather_copy(idx_ref[0, 0, r], r, slot).start()
            return c
        lax.fori_loop(0, mb, body, 0, unroll=8)

    def wait_rows(copy_fn, slot):
        def body(r, c):
            copy_fn(0, r, slot).wait()
            return c
        lax.fori_loop(0, mb, body, 0, unroll=8)

    @pl.when(b < n_used)
    def _active():
        @pl.when(b == 0)
        def _():
            start_gather(tok_ref, 0)
            ybuf[1] = jnp.zeros(ybuf.shape[1:], ybuf.dtype)
            for half in range(2):
                fill = pltpu.make_async_copy(
                    ybuf.at[1], y_hbm.at[pl.ds(n_rows + half * mb, mb), :], ssem.at[1])
                fill.start()
                fill.wait()

        @pl.when(b + 1 < n_used)
        def _():
            start_gather(tok_next_ref, 1 - cur)

        first_of_expert = jnp.logical_or(b == 0, blk_e_ref[b] != blk_e_ref[jnp.maximum(b - 1, 0)])

        @pl.when(first_of_expert)
        def _():
            wg_s[...] = wg_ref[...].astype(BF16)
            wu_s[...] = wu_ref[...].astype(BF16)
            wd_s[...] = wd_ref[...].astype(BF16)

        wait_rows(gather_copy, cur)

        @pl.when(b >= 2)
        def _():
            wait_rows(scatter_copy, cur)

        xb = xbuf[cur].astype(BF16)
        gate = jnp.dot(xb, wg_s[...], preferred_element_type=F32)
        up = jnp.dot(xb, wu_s[...], preferred_element_type=F32)
        hid = gate * jax.nn.sigmoid(gate) * up
        y = jnp.dot(hid.astype(BF16), wd_s[...], preferred_element_type=F32)
        ybuf[cur] = y * w_ref[...]

        def body(r, c):
            scatter_copy(slot_ref[0, 0, r], r, cur).start()
            return c
        lax.fori_loop(0, mb, body, 0, unroll=8)

        @pl.when(b == n_used - 1)
        def _drain():
            @pl.when(b >= 1)
            def _():
                wait_rows(scatter_copy, 1 - cur)
            wait_rows(scatter_copy, cur)


def _experts(f2, eid, wts, w_gate, w_up, w_down):
    T, D = f2.shape
    DE = w_gate.shape[-1]
    mb = EXPERT_MB
    N = 2 * T
    flat_e = eid.T.reshape(N)
    flat_w = wts.T.reshape(N)
    order = jnp.argsort(flat_e, stable=True).astype(jnp.int32)
    se = flat_e[order]
    counts = jnp.bincount(flat_e, length=N_EXPERTS).astype(jnp.int32)
    starts = jnp.cumsum(counts) - counts
    padded = (counts + mb - 1) // mb * mb
    pend = jnp.cumsum(padded)
    pstarts = pend - padded
    dest = pstarts[se] + (jnp.arange(N, dtype=jnp.int32) - starts[se])
    P = N + N_EXPERTS * mb
    nblk = P // mb
    tok_buf = jnp.zeros((P,), jnp.int32).at[dest].set(order // 2)
    slot_row = (order % 2) * T + order // 2
    pos = jnp.arange(P, dtype=jnp.int32)
    spare_row = N + ((pos // mb) % 2) * mb + pos % mb
    slot_buf = spare_row.at[dest].set(slot_row)
    w_buf = jnp.zeros((P,), F32).at[dest].set(flat_w[order])
    n_used = (pend[-1] // mb).astype(jnp.int32)
    blk_e = jnp.searchsorted(pend, jnp.arange(nblk, dtype=jnp.int32) * mb, side="right")
    last_e = jnp.searchsorted(pend, (n_used - 1) * mb, side="right")
    blk_e = jnp.minimum(blk_e, last_e).astype(jnp.int32)

    grid_spec = pltpu.PrefetchScalarGridSpec(
        num_scalar_prefetch=2,
        grid=(nblk,),
        in_specs=[
            pl.BlockSpec((1, 1, mb), lambda b, be, nu: (b, 0, 0), memory_space=pltpu.SMEM),
            pl.BlockSpec((1, 1, mb), lambda b, be, nu: (jnp.minimum(b + 1, nblk - 1), 0, 0),
                         memory_space=pltpu.SMEM),
            pl.BlockSpec((1, 1, mb), lambda b, be, nu: (b, 0, 0), memory_space=pltpu.SMEM),
            pl.BlockSpec((mb, 1), lambda b, be, nu: (b, 0)),
            pl.BlockSpec((None, D, DE), lambda b, be, nu: (be[b], 0, 0)),
            pl.BlockSpec((None, D, DE), lambda b, be, nu: (be[b], 0, 0)),
            pl.BlockSpec((None, DE, D), lambda b, be, nu: (be[b], 0, 0)),
            pl.BlockSpec(memory_space=pl.ANY),
        ],
        out_specs=pl.BlockSpec(memory_space=pl.ANY),
        scratch_shapes=[
            pltpu.VMEM((2, mb, D), F32),
            pltpu.VMEM((2, mb, D), F32),
            pltpu.VMEM((D, DE), BF16), pltpu.VMEM((D, DE), BF16), pltpu.VMEM((DE, D), BF16),
            pltpu.SemaphoreType.DMA((2,)), pltpu.SemaphoreType.DMA((2,)),
        ],
    )
    y = pl.pallas_call(
        functools.partial(_expert_kernel, mb=mb, n_rows=N),
        grid_spec=grid_spec,
        out_shape=jax.ShapeDtypeStruct((N + 2 * mb, D), F32),
        compiler_params=pltpu.CompilerParams(
            dimension_semantics=("arbitrary",), vmem_limit_bytes=VMEM_LIMIT),
        name="experts",
    )(blk_e, n_used.reshape(1), tok_buf.reshape(nblk, 1, mb), tok_buf.reshape(nblk, 1, mb),
      slot_buf.reshape(nblk, 1, mb), w_buf.reshape(P, 1), w_gate, w_up, w_down, f2)
    return y


def _ple_kernel(h1_ref, y0_ref, y1_ref, p_ref, gple_ref, wg_ref, bg_ref, wple_ref, gfin_ref, out_ref,
                *, final):
    h = h1_ref[...] + y0_ref[...] + y1_ref[...]
    a = _rms(h, gple_ref[...]).astype(BF16)
    gate = jax.nn.sigmoid(jnp.dot(a, wg_ref[...], preferred_element_type=F32) + bg_ref[...])
    ple = jnp.dot(p_ref[...].astype(BF16), wple_ref[...], preferred_element_type=F32)
    h = h + gate * ple
    out_ref[...] = _rms(h, gfin_ref[...]) if final else h


def _ple(h1, y3, p2, g_ple, w_ple_gate, b_ple_gate, w_ple, g_final, final):
    T, D = h1.shape
    PD = p2.shape[1]
    tm = PLE_TM
    const = lambda shape: pl.BlockSpec(shape, lambda i: (0,) * len(shape))
    return pl.pallas_call(
        functools.partial(_ple_kernel, final=final),
        grid=(T // tm,),
        in_specs=[
            pl.BlockSpec((tm, D), lambda i: (i, 0)),
            pl.BlockSpec((tm, D), lambda i: (i, 0)),
            pl.BlockSpec((tm, D), lambda i: (T // tm + i, 0)),
            pl.BlockSpec((tm, PD), lambda i: (i, 0)),
            const((1, D)), const((D, D)), const((1, D)), const((PD, D)), const((1, D)),
        ],
        out_specs=pl.BlockSpec((tm, D), lambda i: (i, 0)),
        out_shape=jax.ShapeDtypeStruct((T, D), F32),
        compiler_params=pltpu.CompilerParams(
            dimension_semantics=("arbitrary",), vmem_limit_bytes=VMEM_LIMIT),
        name="ple_final",
    )(h1, y3, y3, p2, g_ple.reshape(1, D), w_ple_gate.astype(BF16), b_ple_gate.reshape(1, D),
      w_ple.astype(BF16), g_final.reshape(1, D))


def kernel(x, p, g_mix, w_in, beta_attn, w_pool, pool_scale, w_out, g_ffn, w_router_group,
           b_router_group, w_router_expert, b_router_expert, w_expert_gate, w_expert_up,
           w_expert_down, g_ple, w_ple, w_ple_gate, b_ple_gate, g_final):
    B, S, D = x.shape
    depth = w_in.shape[0]
    T = B * S
    A = N_HEADS * HEAD_DIM
    slopes = jnp.asarray([2.0 ** (-8.0 * (h + 1) / N_HEADS) for h in range(N_HEADS)], F32)
    h = x.reshape(T, D)
    for i in range(depth):
        proj, kmean = _inproj(h, g_mix[i], w_in[i].astype(BF16), HEAD_DIM ** -0.5)
        o = _attention(proj.reshape(B, S, -1), kmean.reshape(B, S // MOBA_BLOCK, A), slopes)
        h1, f, eid, wts = _mix(o.reshape(T, A), proj, h, S, w_pool[i], beta_attn[i], pool_scale[i],
                               w_out[i], g_ffn[i], w_router_group[i], b_router_group[i],
                               w_router_expert[i], b_router_expert[i])
        y3 = _experts(f, eid, wts, w_expert_gate[i], w_expert_up[i], w_expert_down[i])
        h = _ple(h1, y3, p[i].reshape(T, -1), g_ple[i], w_ple_gate[i], b_ple_gate[i], w_ple[i],
                 g_final, final=(i == depth - 1))
    return h.reshape(B, S, D)
```

```python
import functools

import jax
import jax.numpy as jnp
from jax import lax
from jax.experimental import pallas as pl
from jax.experimental.pallas import tpu as pltpu

F32 = jnp.float32
BF16 = jnp.bfloat16

N_HEADS = 8
HEAD_DIM = 128
MOBA_BLOCK = 256
MOBA_TOPK = 3
POOL_WINDOWS = (2, 4, 8, 16)
POOL_GROUP = 256
N_GROUPS = 8
EXPERTS_PER_GROUP = 8
N_EXPERTS = N_GROUPS * EXPERTS_PER_GROUP
EPS = 1e-6

LANES = 128
NEG = -1e30
VMEM_LIMIT = 56 * 1024 * 1024

INPROJ_TM = 512
MIX_TM = 256
POOL_HALO = 128
EXPERT_MB = 256
PLE_TM = 256
ATTN_HEADS_PER_STEP = 2
V_PAD_ROWS = 16

_NT = (((1,), (1,)), ((), ()))


def _rms(x, gain):
    return x * lax.rsqrt(jnp.mean(x * x, axis=-1, keepdims=True) + EPS) * gain


def _inproj_kernel(x_ref, g_ref, w_ref, proj_ref, kmean_ref, a_ref, *, tm, scale):
    j = pl.program_id(1)

    @pl.when(j == 0)
    def _():
        a_ref[...] = _rms(x_ref[...], g_ref[...]).astype(BF16)

    acc = jnp.dot(a_ref[...], w_ref[...], preferred_element_type=F32)
    proj_ref[...] = (acc * jnp.where(j == 0, scale, 1.0)).astype(BF16)

    @pl.when(j == 1)
    def _():
        kmean_ref[0] = jnp.mean(acc.reshape(tm // MOBA_BLOCK, MOBA_BLOCK, acc.shape[-1]), axis=1)


def _inproj(x2, g_mix, w_in_bf16, scale):
    T, D = x2.shape
    A = N_HEADS * HEAD_DIM
    tm = INPROJ_TM
    n_chunks = w_in_bf16.shape[1] // A
    return pl.pallas_call(
        functools.partial(_inproj_kernel, tm=tm, scale=scale),
        grid=(T // tm, n_chunks),
        in_specs=[
            pl.BlockSpec((tm, D), lambda i, j: (i, 0)),
            pl.BlockSpec((1, D), lambda i, j: (0, 0)),
            pl.BlockSpec((D, A), lambda i, j: (0, j)),
        ],
        out_specs=[
            pl.BlockSpec((tm, A), lambda i, j: (i, j)),
            pl.BlockSpec((1, tm // MOBA_BLOCK, A), lambda i, j: (i, 0, 0)),
        ],
        out_shape=[
            jax.ShapeDtypeStruct((T, n_chunks * A), BF16),
            jax.ShapeDtypeStruct((T // tm, tm // MOBA_BLOCK, A), F32),
        ],
        scratch_shapes=[pltpu.VMEM((tm, D), BF16)],
        compiler_params=pltpu.CompilerParams(
            dimension_semantics=("arbitrary", "arbitrary"), vmem_limit_bytes=VMEM_LIMIT),
        name="inproj",
    )(x2, g_mix.reshape(1, D), w_in_bf16)


def _attn_kernel(slopes_ref, q_ref, k_ref, v_ref, km_ref, o_ref,
                 kaug_ref, vt_ref, qaug_ref, s_ref, bmax_ref, m_ref, acc_ref, *, nb, hpb):
    hg = pl.program_id(1)
    i = pl.program_id(2)
    blk_rows = MOBA_BLOCK
    n_feat = LANES

    @pl.when(i == 0)
    def _build_kv():
        for hh in range(hpb):
            slope = slopes_ref[hg * hpb + hh]
            cs = slice(hh * HEAD_DIM, (hh + 1) * HEAD_DIM)

            def body(j, c, hh=hh, slope=slope, cs=cs):
                r0 = pl.multiple_of(j * blk_rows, blk_rows)
                kaug_ref[hh, j, :, 0:HEAD_DIM] = k_ref[pl.ds(r0, blk_rows), cs]
                lane = lax.broadcasted_iota(jnp.int32, (blk_rows, n_feat), 1)
                kj = lax.broadcasted_iota(jnp.int32, (blk_rows, n_feat), 0).astype(F32)
                feat = jnp.where(lane == j, 1.0,
                                 jnp.where(lane == nb, 1.0,
                                           jnp.where(lane == nb + 1, slope * kj, 0.0)))
                kaug_ref[hh, j, :, HEAD_DIM:HEAD_DIM + n_feat] = feat.astype(BF16)
                vt_ref[hh, j, 0:HEAD_DIM, :] = v_ref[pl.ds(r0, blk_rows), cs].astype(F32).T.astype(BF16)
                ones_row = lax.broadcasted_iota(jnp.int32, (V_PAD_ROWS, blk_rows), 0) == 0
                vt_ref[hh, j, HEAD_DIM:HEAD_DIM + V_PAD_ROWS, :] = jnp.where(ones_row, 1.0, 0.0).astype(BF16)
                return c
            lax.fori_loop(0, nb, body, 0)

    blk = lax.broadcasted_iota(jnp.int32, (nb, blk_rows), 0)
    row2 = lax.broadcasted_iota(jnp.int32, (n_feat - nb, blk_rows), 0)
    qi = lax.broadcasted_iota(jnp.int32, (n_feat - nb, blk_rows), 1).astype(F32)
    for hh in range(hpb):
        slope = slopes_ref[hg * hpb + hh]
        cs = slice(hh * HEAD_DIM, (hh + 1) * HEAD_DIM)
        q_t = q_ref[:, cs].astype(F32).T.astype(BF16)
        km = km_ref[:, cs]
        km_hi = km.astype(BF16)
        km_lo = (km - km_hi.astype(F32)).astype(BF16)
        g = (jnp.dot(km_hi, q_t, preferred_element_type=F32)
             + jnp.dot(km_lo, q_t, preferred_element_type=F32))
        g = jnp.where(blk < i, g, -jnp.inf)
        sel = blk == i
        for r in range(min(MOBA_TOPK, nb)):
            mx = jnp.max(g, axis=0, keepdims=True)
            idx = jnp.min(jnp.where(g == mx, blk, nb), axis=0, keepdims=True)
            pick = jnp.logical_and(blk == idx, r < i)
            sel = jnp.logical_or(sel, pick)
            g = jnp.where(pick, -jnp.inf, g)
        bias = jnp.where(sel, (-slope * blk_rows) * (i - blk).astype(F32), NEG)
        extra = jnp.where(row2 == 0, -slope * qi, jnp.where(row2 == 1, 1.0, 0.0))
        qaug_ref[hh, 0:HEAD_DIM, :] = q_t
        qaug_ref[hh, HEAD_DIM:HEAD_DIM + nb, :] = bias.astype(BF16)
        qaug_ref[hh, HEAD_DIM + nb:HEAD_DIM + n_feat, :] = extra.astype(BF16)

    def block_at(t):
        return jnp.where(t == 0, i, jnp.where(t <= i, t - 1, jnp.minimum(i + 1, nb - 1)))

    def scores(slot, b, causal):
        for hh in range(hpb):
            s = jnp.dot(kaug_ref[hh, b], qaug_ref[hh], preferred_element_type=F32)
            if causal:
                key_i = lax.broadcasted_iota(jnp.int32, (blk_rows, blk_rows), 0)
                qry_i = lax.broadcasted_iota(jnp.int32, (blk_rows, blk_rows), 1)
                s = jnp.where(key_i <= qry_i, s, NEG)
            s_ref[slot, hh] = s
            bmax_ref[slot, hh] = jnp.max(s, axis=0, keepdims=True)

    def accumulate(slot, b):
        for hh in range(hpb):
            m_prev = m_ref[hh]
            m_new = jnp.maximum(m_prev, bmax_ref[slot, hh])
            alpha = jnp.exp(m_prev - m_new)
            p = jnp.exp(s_ref[slot, hh] - m_new)
            acc_ref[hh] = alpha * acc_ref[hh] + jnp.dot(
                vt_ref[hh, b], p.astype(BF16), preferred_element_type=F32)
            m_ref[hh] = m_new

    m_ref[...] = jnp.full(m_ref.shape, -jnp.inf, F32)
    acc_ref[...] = jnp.zeros(acc_ref.shape, F32)
    scores(0, i, causal=True)

    def body(t, c):
        b1 = block_at(2 * t + 1)
        scores(1, b1, causal=False)
        accumulate(0, block_at(2 * t))
        scores(0, block_at(2 * t + 2), causal=False)
        accumulate(1, b1)
        return c
    lax.fori_loop(0, (i + 2) // 2, body, 0)

    for hh in range(hpb):
        o_t = acc_ref[hh, 0:HEAD_DIM, :] / acc_ref[hh, HEAD_DIM:HEAD_DIM + 1, :]
        o_ref[:, hh * HEAD_DIM:(hh + 1) * HEAD_DIM] = o_t.T.astype(o_ref.dtype)


def _attention(proj3, kmean3, slopes):
    B, S, _ = proj3.shape
    nb = S // MOBA_BLOCK
    hpb = ATTN_HEADS_PER_STEP
    assert S % MOBA_BLOCK == 0 and nb % 16 == 0 and nb + 2 <= LANES and N_HEADS % hpb == 0
    A = N_HEADS * HEAD_DIM
    w = hpb * HEAD_DIM
    n_hg = N_HEADS // hpb
    grid_spec = pltpu.PrefetchScalarGridSpec(
        num_scalar_prefetch=1,
        grid=(B, n_hg, nb),
        in_specs=[
            pl.BlockSpec((None, MOBA_BLOCK, w), lambda b, h, i, s: (b, i, h)),
            pl.BlockSpec((None, S, w), lambda b, h, i, s: (b, 0, n_hg + h)),
            pl.BlockSpec((None, S, w), lambda b, h, i, s: (b, 0, 2 * n_hg + h)),
            pl.BlockSpec((None, nb, w), lambda b, h, i, s: (b, 0, h)),
        ],
        out_specs=pl.BlockSpec((None, MOBA_BLOCK, w), lambda b, h, i, s: (b, i, h)),
        scratch_shapes=[
            pltpu.VMEM((hpb, nb, MOBA_BLOCK, HEAD_DIM + LANES), BF16),
            pltpu.VMEM((hpb, nb, HEAD_DIM + V_PAD_ROWS, MOBA_BLOCK), BF16),
            pltpu.VMEM((hpb, HEAD_DIM + LANES, MOBA_BLOCK), BF16),
            pltpu.VMEM((2, hpb, MOBA_BLOCK, MOBA_BLOCK), F32),
            pltpu.VMEM((2, hpb, 1, MOBA_BLOCK), F32),
            pltpu.VMEM((hpb, 1, MOBA_BLOCK), F32),
            pltpu.VMEM((hpb, HEAD_DIM + V_PAD_ROWS, MOBA_BLOCK), F32),
        ],
    )
    return pl.pallas_call(
        functools.partial(_attn_kernel, nb=nb, hpb=hpb),
        grid_spec=grid_spec,
        out_shape=jax.ShapeDtypeStruct((B, S, A), BF16),
        compiler_params=pltpu.CompilerParams(
            dimension_semantics=("arbitrary", "arbitrary", "arbitrary"),
            vmem_limit_bytes=VMEM_LIMIT),
        name="moba_attn",
    )(slopes, proj3, proj3, proj3, kmean3)


def _mix_kernel(o_ref, u_ref, halo_ref, band_ref, wpool_ref, beta_ref, pscale_ref, wout_ref,
                x_ref, gffn_ref, wr_hi_ref, wr_lo_ref, br_ref,
                h1_ref, f_ref, eid_ref, wts_ref, *, tm, seq):
    i = pl.program_id(0)
    t0 = (i * tm) % seq
    u = u_ref[...]
    halo = jnp.where(t0 == 0, jnp.zeros_like(halo_ref[...]), halo_ref[...])
    uext = jnp.concatenate([halo, u], axis=0)
    t = t0 + lax.broadcasted_iota(jnp.int32, (tm, 1), 0)
    zs = []
    for g, w in enumerate(POOL_WINDOWS):
        cs = slice(g * POOL_GROUP, (g + 1) * POOL_GROUP)
        wsum = jnp.dot(band_ref[g], uext[:, cs], preferred_element_type=F32)
        cnt = jnp.minimum(t + 1, w).astype(F32)
        z = wsum / cnt - u[:, cs].astype(F32)
        zs.append(jnp.dot(z.astype(BF16), wpool_ref[g], preferred_element_type=F32))
    o_pool = jnp.concatenate(zs, axis=1)
    mixed = jnp.concatenate(
        [_rms(o_ref[...].astype(F32), beta_ref[...]).astype(BF16),
         _rms(o_pool, pscale_ref[...]).astype(BF16)], axis=1)
    h1 = x_ref[...] + jnp.dot(mixed, wout_ref[...], preferred_element_type=F32)
    h1_ref[...] = h1
    f = _rms(h1, gffn_ref[...])
    f_ref[...] = f

    f_hi = f.astype(BF16)
    f_lo = (f - f_hi.astype(F32)).astype(BF16)
    wr_hi = wr_hi_ref[...]
    logits = (jnp.dot(f_hi, wr_hi, preferred_element_type=F32)
              + jnp.dot(f_lo, wr_hi, preferred_element_type=F32)
              + jnp.dot(f_hi, wr_lo_ref[...], preferred_element_type=F32))
    lt = logits.T + br_ref[...]
    row8 = lax.broadcasted_iota(jnp.int32, (N_GROUPS, tm), 0)
    gl = lt[0:N_GROUPS]
    gmax = jnp.max(gl, axis=0, keepdims=True)
    gidx = jnp.min(jnp.where(gl == gmax, row8, N_GROUPS), axis=0, keepdims=True)
    g_val = 1.0 / jnp.sum(jnp.exp(gl - gmax), axis=0, keepdims=True)
    el = jnp.zeros((EXPERTS_PER_GROUP, tm), F32)
    for g in range(N_GROUPS):
        lo = N_GROUPS + g * EXPERTS_PER_GROUP
        el = jnp.where(gidx == g, lt[lo:lo + EXPERTS_PER_GROUP], el)
    emax = jnp.max(el, axis=0, keepdims=True)
    esum = jnp.sum(jnp.exp(el - emax), axis=0, keepdims=True)
    i1 = jnp.min(jnp.where(el == emax, row8, EXPERTS_PER_GROUP), axis=0, keepdims=True)
    el2 = jnp.where(row8 == i1, -jnp.inf, el)
    m2 = jnp.max(el2, axis=0, keepdims=True)
    i2 = jnp.min(jnp.where(el2 == m2, row8, EXPERTS_PER_GROUP), axis=0, keepdims=True)
    p1 = 1.0 / esum
    p2 = jnp.exp(m2 - emax) / esum
    den = p1 + p2
    eid_ref[0:1, :] = gidx * EXPERTS_PER_GROUP + i1
    eid_ref[1:2, :] = gidx * EXPERTS_PER_GROUP + i2
    wts_ref[0:1, :] = g_val * (p1 / den)
    wts_ref[1:2, :] = g_val * (p2 / den)


def _pool_band(tm):
    r = jnp.arange(tm)[:, None]
    c = jnp.arange(POOL_HALO + tm)[None, :] - POOL_HALO
    return jnp.stack([((c <= r) & (c >= r - w + 1)).astype(BF16) for w in POOL_WINDOWS])


def _mix(o2, proj, x2, seq, w_pool, beta_attn, pool_scale, w_out, g_ffn, w_rg, b_rg, w_re, b_re):
    T, D = x2.shape
    A = N_HEADS * HEAD_DIM
    PW = len(POOL_WINDOWS) * POOL_GROUP
    tm = MIX_TM
    assert seq % tm == 0 and tm % POOL_HALO == 0 and POOL_HALO >= max(POOL_WINDOWS) - 1
    u_col = (proj.shape[1] - PW) // PW
    halo_per_tile = tm // POOL_HALO
    wr = jnp.concatenate([w_rg, jnp.transpose(w_re, (1, 0, 2)).reshape(D, N_EXPERTS)], axis=1)
    wr = jnp.pad(wr, ((0, 0), (0, LANES - wr.shape[1])))
    wr_hi = wr.astype(BF16)
    wr_lo = (wr - wr_hi.astype(F32)).astype(BF16)
    br = jnp.pad(jnp.concatenate([b_rg, b_re.reshape(-1)]), (0, LANES - N_GROUPS - N_EXPERTS))
    const = lambda shape: pl.BlockSpec(shape, lambda i: (0,) * len(shape))
    return pl.pallas_call(
        functools.partial(_mix_kernel, tm=tm, seq=seq),
        grid=(T // tm,),
        in_specs=[
            pl.BlockSpec((tm, A), lambda i: (i, 0)),
            pl.BlockSpec((tm, PW), lambda i: (i, u_col)),
            pl.BlockSpec((POOL_HALO, PW),
                         lambda i: (jnp.maximum(i * halo_per_tile - 1, 0), u_col)),
            const((len(POOL_WINDOWS), tm, POOL_HALO + tm)),
            const((len(POOL_WINDOWS), POOL_GROUP, POOL_GROUP)),
            const((1, A)), const((1, PW)), const((A + PW, D)),
            pl.BlockSpec((tm, D), lambda i: (i, 0)),
            const((1, D)), const((D, LANES)), const((D, LANES)), const((LANES, 1)),
        ],
        out_specs=[
            pl.BlockSpec((tm, D), lambda i: (i, 0)),
            pl.BlockSpec((tm, D), lambda i: (i, 0)),
            pl.BlockSpec((2, tm), lambda i: (0, i)),
            pl.BlockSpec((2, tm), lambda i: (0, i)),
        ],
        out_shape=[
            jax.ShapeDtypeStruct((T, D), F32),
            jax.ShapeDtypeStruct((T, D), F32),
            jax.ShapeDtypeStruct((2, T), jnp.int32),
            jax.ShapeDtypeStruct((2, T), F32),
        ],
        compiler_params=pltpu.CompilerParams(
            dimension_semantics=("arbitrary",), vmem_limit_bytes=VMEM_LIMIT),
        name="mix_router",
    )(o2, proj, proj, _pool_band(tm), w_pool.astype(BF16), beta_attn.reshape(1, A),
      pool_scale.reshape(1, PW), w_out.astype(BF16), x2, g_ffn.reshape(1, D),
      wr_hi, wr_lo, br.reshape(LANES, 1))


def _expert_kernel(blk_e_ref, n_used_ref, tok_ref, tok_next_ref, slot_ref, w_ref,
                   wg_ref, wu_ref, wd_ref, f_hbm, y_hbm,
                   xbuf, ybuf, wg_s, wu_s, wd_s, gsem, ssem, *, mb, n_rows):
    b = pl.program_id(0)
    n_used = n_used_ref[0]
    cur = b % 2

    def gather_copy(tok, r, slot):
        return pltpu.make_async_copy(f_hbm.at[pl.ds(tok, 1), :],
                                     xbuf.at[slot, pl.ds(r, 1), :], gsem.at[slot])

    def scatter_copy(dst, r, slot):
        return pltpu.make_async_copy(ybuf.at[slot, pl.ds(r, 1), :],
                                     y_hbm.at[pl.ds(dst, 1), :], ssem.at[slot])

    def start_gather(idx_ref, slot):
        def body(r, c):
            gather_copy(idx_ref[0, 0, r], r, slot).start()
            return c
        lax.fori_loop(0, mb, body, 0, unroll=8)

    def wait_rows(copy_fn, slot):
        def body(r, c):
            copy_fn(0, r, slot).wait()
            return c
        lax.fori_loop(0, mb, body, 0, unroll=8)

    @pl.when(b < n_used)
    def _active():
        @pl.when(b == 0)
        def _():
            start_gather(tok_ref, 0)
            ybuf[1] = jnp.zeros(ybuf.shape[1:], ybuf.dtype)
            for half in range(2):
                fill = pltpu.make_async_copy(
                    ybuf.at[1], y_hbm.at[pl.ds(n_rows + half * mb, mb), :], ssem.at[1])
                fill.start()
                fill.wait()

        @pl.when(b + 1 < n_used)
        def _():
            start_gather(tok_next_ref, 1 - cur)

        first_of_expert = jnp.logical_or(b == 0, blk_e_ref[b] != blk_e_ref[jnp.maximum(b - 1, 0)])

        @pl.when(first_of_expert)
        def _():
            wg_s[...] = wg_ref[...].astype(BF16)
            wu_s[...] = wu_ref[...].astype(BF16)
            wd_s[...] = wd_ref[...].astype(BF16)

        wait_rows(gather_copy, cur)

        @pl.when(b >= 2)
        def _():
            wait_rows(scatter_copy, cur)

        xb = xbuf[cur].astype(BF16)
        gate = jnp.dot(xb, wg_s[...], preferred_element_type=F32)
        up = jnp.dot(xb, wu_s[...], preferred_element_type=F32)
        hid = gate * jax.nn.sigmoid(gate) * up
        y = jnp.dot(hid.astype(BF16), wd_s[...], preferred_element_type=F32)
        ybuf[cur] = y * w_ref[...]

        def body(r, c):
            scatter_copy(slot_ref[0, 0, r], r, cur).start()
            return c
        lax.fori_loop(0, mb, body, 0, unroll=8)

        @pl.when(b == n_used - 1)
        def _drain():
            @pl.when(b >= 1)
            def _():
                wait_rows(scatter_copy, 1 - cur)
            wait_rows(scatter_copy, cur)


def _experts(f2, eid, wts, w_gate, w_up, w_down):
    T, D = f2.shape
    DE = w_gate.shape[-1]
    mb = EXPERT_MB
    N = 2 * T
    flat_e = eid.T.reshape(N)
    flat_w = wts.T.reshape(N)
    order = jnp.argsort(flat_e, stable=True).astype(jnp.int32)
    experts = jnp.arange(N_EXPERTS, dtype=jnp.int32)
    counts = jnp.sum((flat_e[:, None] == experts[None, :]).astype(jnp.int32), axis=0)
    starts = jnp.cumsum(counts) - counts
    padded = (counts + mb - 1) // mb * mb
    pend = jnp.cumsum(padded)
    pstarts = pend - padded
    P = N + N_EXPERTS * mb
    nblk = P // mb
    n_used = (pend[-1] // mb).astype(jnp.int32)
    blk_start = jnp.arange(nblk, dtype=jnp.int32) * mb
    blk_e = jnp.sum((pend[None, :] <= blk_start[:, None]).astype(jnp.int32), axis=1)
    last_e = jnp.max(jnp.where(counts > 0, experts, 0))
    blk_e = jnp.minimum(blk_e, last_e)
    r = jnp.arange(mb, dtype=jnp.int32)[None, :]
    off = (blk_start - pstarts[blk_e])[:, None] + r
    valid = off < counts[blk_e][:, None]
    slot = order[jnp.where(valid, starts[blk_e][:, None] + off, 0)]
    spare_row = N + (jnp.arange(nblk, dtype=jnp.int32)[:, None] % 2) * mb + r
    tok_buf = jnp.where(valid, slot // 2, 0)
    slot_buf = jnp.where(valid, (slot % 2) * T + slot // 2, spare_row)
    w_buf = jnp.where(valid, flat_w[slot], 0.0)

    grid_spec = pltpu.PrefetchScalarGridSpec(
        num_scalar_prefetch=2,
        grid=(nblk,),
        in_specs=[
            pl.BlockSpec((1, 1, mb), lambda b, be, nu: (b, 0, 0), memory_space=pltpu.SMEM),
            pl.BlockSpec((1, 1, mb), lambda b, be, nu: (jnp.minimum(b + 1, nblk - 1), 0, 0),
                         memory_space=pltpu.SMEM),
            pl.BlockSpec((1, 1, mb), lambda b, be, nu: (b, 0, 0), memory_space=pltpu.SMEM),
            pl.BlockSpec((mb, 1), lambda b, be, nu: (b, 0)),
            pl.BlockSpec((None, D, DE), lambda b, be, nu: (be[b], 0, 0)),
            pl.BlockSpec((None, D, DE), lambda b, be, nu: (be[b], 0, 0)),
            pl.BlockSpec((None, DE, D), lambda b, be, nu: (be[b], 0, 0)),
            pl.BlockSpec(memory_space=pl.ANY),
        ],
        out_specs=pl.BlockSpec(memory_space=pl.ANY),
        scratch_shapes=[
            pltpu.VMEM((2, mb, D), F32),
            pltpu.VMEM((2, mb, D), F32),
            pltpu.VMEM((D, DE), BF16), pltpu.VMEM((D, DE), BF16), pltpu.VMEM((DE, D), BF16),
            pltpu.SemaphoreType.DMA((2,)), pltpu.SemaphoreType.DMA((2,)),
        ],
    )
    y = pl.pallas_call(
        functools.partial(_expert_kernel, mb=mb, n_rows=N),
        grid_spec=grid_spec,
        out_shape=jax.ShapeDtypeStruct((N + 2 * mb, D), F32),
        compiler_params=pltpu.CompilerParams(
            dimension_semantics=("arbitrary",), vmem_limit_bytes=VMEM_LIMIT),
        name="experts",
    )(blk_e, n_used.reshape(1), tok_buf.reshape(nblk, 1, mb), tok_buf.reshape(nblk, 1, mb),
      slot_buf.reshape(nblk, 1, mb), w_buf.reshape(P, 1), w_gate, w_up, w_down, f2)
    return y


def _ple_kernel(h1_ref, y0_ref, y1_ref, p_ref, gple_ref, wg_ref, bg_ref, wple_ref, gfin_ref, out_ref,
                *, final):
    h = h1_ref[...] + y0_ref[...] + y1_ref[...]
    a = _rms(h, gple_ref[...]).astype(BF16)
    gate = jax.nn.sigmoid(jnp.dot(a, wg_ref[...], preferred_element_type=F32) + bg_ref[...])
    ple = jnp.dot(p_ref[...].astype(BF16), wple_ref[...], preferred_element_type=F32)
    h = h + gate * ple
    out_ref[...] = _rms(h, gfin_ref[...]) if final else h


def _ple(h1, y3, p2, g_ple, w_ple_gate, b_ple_gate, w_ple, g_final, final):
    T, D = h1.shape
    PD = p2.shape[1]
    tm = PLE_TM
    const = lambda shape: pl.BlockSpec(shape, lambda i: (0,) * len(shape))
    return pl.pallas_call(
        functools.partial(_ple_kernel, final=final),
        grid=(T // tm,),
        in_specs=[
            pl.BlockSpec((tm, D), lambda i: (i, 0)),
            pl.BlockSpec((tm, D), lambda i: (i, 0)),
            pl.BlockSpec((tm, D), lambda i: (T // tm + i, 0)),
            pl.BlockSpec((tm, PD), lambda i: (i, 0)),
            const((1, D)), const((D, D)), const((1, D)), const((PD, D)), const((1, D)),
        ],
        out_specs=pl.BlockSpec((tm, D), lambda i: (i, 0)),
        out_shape=jax.ShapeDtypeStruct((T, D), F32),
        compiler_params=pltpu.CompilerParams(
            dimension_semantics=("arbitrary",), vmem_limit_bytes=VMEM_LIMIT),
        name="ple_final",
    )(h1, y3, y3, p2, g_ple.reshape(1, D), w_ple_gate.astype(BF16), b_ple_gate.reshape(1, D),
      w_ple.astype(BF16), g_final.reshape(1, D))


def kernel(x, p, g_mix, w_in, beta_attn, w_pool, pool_scale, w_out, g_ffn, w_router_group,
           b_router_group, w_router_expert, b_router_expert, w_expert_gate, w_expert_up,
           w_expert_down, g_ple, w_ple, w_ple_gate, b_ple_gate, g_final):
    B, S, D = x.shape
    depth = w_in.shape[0]
    T = B * S
    A = N_HEADS * HEAD_DIM
    slopes = jnp.asarray([2.0 ** (-8.0 * (h + 1) / N_HEADS) for h in range(N_HEADS)], F32)
    h = x.reshape(T, D)
    for i in range(depth):
        proj, kmean = _inproj(h, g_mix[i], w_in[i].astype(BF16), HEAD_DIM ** -0.5)
        o = _attention(proj.reshape(B, S, -1), kmean.reshape(B, S // MOBA_BLOCK, A), slopes)
        h1, f, eid, wts = _mix(o.reshape(T, A), proj, h, S, w_pool[i], beta_attn[i], pool_scale[i],
                               w_out[i], g_ffn[i], w_router_group[i], b_router_group[i],
                               w_router_expert[i], b_router_expert[i])
        y3 = _experts(f, eid, wts, w_expert_gate[i], w_expert_up[i], w_expert_down[i])
        h = _ple(h1, y3, p[i].reshape(T, -1), g_ple[i], w_ple_gate[i], b_ple_gate[i], w_ple[i],
                 g_final, final=(i == depth - 1))
    return h.reshape(B, S, D)
```

```python
import functools

import jax
import jax.numpy as jnp
from jax import lax
from jax.experimental import pallas as pl
from jax.experimental.pallas import tpu as pltpu

F32 = jnp.float32
BF16 = jnp.bfloat16

N_HEADS = 8
HEAD_DIM = 128
MOBA_BLOCK = 256
MOBA_TOPK = 3
POOL_WINDOWS = (2, 4, 8, 16)
POOL_GROUP = 256
N_GROUPS = 8
EXPERTS_PER_GROUP = 8
N_EXPERTS = N_GROUPS * EXPERTS_PER_GROUP
EPS = 1e-6

LANES = 128
NEG = -1e30
VMEM_LIMIT = 56 * 1024 * 1024

INPROJ_TM = 512
MIX_TM = 256
POOL_HALO = 128
EXPERT_MB = 256
PLE_TM = 256
ATTN_HEADS_PER_STEP = 2
V_PAD_ROWS = 16
ATTN_FLAGS = None

_NT = (((1,), (1,)), ((), ()))


def _rms(x, gain):
    return x * lax.rsqrt(jnp.mean(x * x, axis=-1, keepdims=True) + EPS) * gain


def _inproj_kernel(x_ref, g_ref, w_ref, proj_ref, kmean_ref, a_ref, *, tm, scale):
    j = pl.program_id(1)

    @pl.when(j == 0)
    def _():
        a_ref[...] = _rms(x_ref[...], g_ref[...]).astype(BF16)

    acc = jnp.dot(a_ref[...], w_ref[...], preferred_element_type=F32)
    proj_ref[...] = (acc * jnp.where(j == 0, scale, 1.0)).astype(BF16)

    @pl.when(j == 1)
    def _():
        kmean_ref[0] = jnp.mean(acc.reshape(tm // MOBA_BLOCK, MOBA_BLOCK, acc.shape[-1]), axis=1)


def _inproj(x2, g_mix, w_in_bf16, scale):
    T, D = x2.shape
    A = N_HEADS * HEAD_DIM
    tm = INPROJ_TM
    n_chunks = w_in_bf16.shape[1] // A
    return pl.pallas_call(
        functools.partial(_inproj_kernel, tm=tm, scale=scale),
        grid=(T // tm, n_chunks),
        in_specs=[
            pl.BlockSpec((tm, D), lambda i, j: (i, 0)),
            pl.BlockSpec((1, D), lambda i, j: (0, 0)),
            pl.BlockSpec((D, A), lambda i, j: (0, j)),
        ],
        out_specs=[
            pl.BlockSpec((tm, A), lambda i, j: (i, j)),
            pl.BlockSpec((1, tm // MOBA_BLOCK, A), lambda i, j: (i, 0, 0)),
        ],
        out_shape=[
            jax.ShapeDtypeStruct((T, n_chunks * A), BF16),
            jax.ShapeDtypeStruct((T // tm, tm // MOBA_BLOCK, A), F32),
        ],
        scratch_shapes=[pltpu.VMEM((tm, D), BF16)],
        compiler_params=pltpu.CompilerParams(
            dimension_semantics=("arbitrary", "arbitrary"), vmem_limit_bytes=VMEM_LIMIT),
        name="inproj",
    )(x2, g_mix.reshape(1, D), w_in_bf16)


def _attn_kernel(slopes_ref, q_ref, k_ref, v_ref, km_ref, o_ref,
                 kaug_ref, vt_ref, qaug_ref, s_ref, bmax_ref, p_ref, alpha_ref, m_ref, acc_ref,
                 *, nb, hpb, tq):
    hg = pl.program_id(1)
    i = pl.program_id(2)
    blk_rows = MOBA_BLOCK
    n_feat = LANES

    @pl.when(i == 0)
    def _build_kv():
        for hh in range(hpb):
            slope = slopes_ref[hg * hpb + hh]
            cs = slice(hh * HEAD_DIM, (hh + 1) * HEAD_DIM)

            def body(j, c, hh=hh, slope=slope, cs=cs):
                r0 = pl.multiple_of(j * blk_rows, blk_rows)
                kaug_ref[hh, j, :, 0:HEAD_DIM] = k_ref[pl.ds(r0, blk_rows), cs]
                lane = lax.broadcasted_iota(jnp.int32, (blk_rows, n_feat), 1)
                kj = lax.broadcasted_iota(jnp.int32, (blk_rows, n_feat), 0).astype(F32)
                feat = jnp.where(lane == j, 1.0,
                                 jnp.where(lane == nb, 1.0,
                                           jnp.where(lane == nb + 1, slope * kj, 0.0)))
                kaug_ref[hh, j, :, HEAD_DIM:HEAD_DIM + n_feat] = feat.astype(BF16)
                vt_ref[hh, j, 0:HEAD_DIM, :] = v_ref[pl.ds(r0, blk_rows), cs].astype(F32).T.astype(BF16)
                ones_row = lax.broadcasted_iota(jnp.int32, (V_PAD_ROWS, blk_rows), 0) == 0
                vt_ref[hh, j, HEAD_DIM:HEAD_DIM + V_PAD_ROWS, :] = jnp.where(ones_row, 1.0, 0.0).astype(BF16)
                return c
            lax.fori_loop(0, nb, body, 0)

    blk = lax.broadcasted_iota(jnp.int32, (nb, tq), 0)
    own = 2 * i + lax.broadcasted_iota(jnp.int32, (nb, tq), 1) // blk_rows
    row2 = lax.broadcasted_iota(jnp.int32, (n_feat - nb, tq), 0)
    qi = (lax.broadcasted_iota(jnp.int32, (n_feat - nb, tq), 1) % blk_rows).astype(F32)
    for hh in range(hpb):
        slope = slopes_ref[hg * hpb + hh]
        cs = slice(hh * HEAD_DIM, (hh + 1) * HEAD_DIM)
        q_t = q_ref[:, cs].astype(F32).T.astype(BF16)
        km = km_ref[:, cs]
        km_hi = km.astype(BF16)
        km_lo = (km - km_hi.astype(F32)).astype(BF16)
        g = (jnp.dot(km_hi, q_t, preferred_element_type=F32)
             + jnp.dot(km_lo, q_t, preferred_element_type=F32))
        g = jnp.where(blk < own, g, -jnp.inf)
        sel = blk == own
        for r in range(min(MOBA_TOPK, nb)):
            mx = jnp.max(g, axis=0, keepdims=True)
            idx = jnp.min(jnp.where(g == mx, blk, nb), axis=0, keepdims=True)
            pick = jnp.logical_and(blk == idx, r < own)
            sel = jnp.logical_or(sel, pick)
            g = jnp.where(pick, -jnp.inf, g)
        bias = jnp.where(sel, (-slope * blk_rows) * (own - blk).astype(F32), NEG)
        extra = jnp.where(row2 == 0, -slope * qi, jnp.where(row2 == 1, 1.0, 0.0))
        qaug_ref[hh, 0:HEAD_DIM, :] = q_t
        qaug_ref[hh, HEAD_DIM:HEAD_DIM + nb, :] = bias.astype(BF16)
        qaug_ref[hh, HEAD_DIM + nb:HEAD_DIM + n_feat, :] = extra.astype(BF16)

    n_visit = 2 * i + 2

    def block_at(t):
        return jnp.where(t < 2, 2 * i + t, jnp.minimum(t - 2, nb - 1))

    def scores(slot, b, diag=None):
        for hh in range(hpb):
            s = jnp.dot(kaug_ref[hh, b], qaug_ref[hh], preferred_element_type=F32)
            if diag is not None:
                key_pos = lax.broadcasted_iota(jnp.int32, (blk_rows, tq), 0) + diag * blk_rows
                qry_pos = lax.broadcasted_iota(jnp.int32, (blk_rows, tq), 1)
                s = jnp.where(key_pos <= qry_pos, s, NEG)
            s_ref[slot, hh] = s
            bmax_ref[slot, hh] = jnp.max(s, axis=0, keepdims=True)

    def softmax(sslot, pslot):
        for hh in range(hpb):
            m_prev = m_ref[hh]
            m_new = jnp.maximum(m_prev, bmax_ref[sslot, hh])
            alpha_ref[pslot, hh] = jnp.exp(m_prev - m_new)
            p_ref[pslot, hh] = jnp.exp(s_ref[sslot, hh] - m_new).astype(BF16)
            m_ref[hh] = m_new

    def weighted_values(pslot, b):
        for hh in range(hpb):
            acc_ref[hh] = alpha_ref[pslot, hh] * acc_ref[hh] + jnp.dot(
                vt_ref[hh, b], p_ref[pslot, hh], preferred_element_type=F32)

    m_ref[...] = jnp.full(m_ref.shape, -jnp.inf, F32)
    acc_ref[...] = jnp.zeros(acc_ref.shape, F32)
    scores(0, 2 * i, diag=0)
    scores(1, 2 * i + 1, diag=1)
    softmax(0, 1)

    def body(t, c):
        scores(0, block_at(2 * t + 2))
        softmax(1, 0)
        weighted_values(1, block_at(2 * t))
        scores(1, block_at(2 * t + 3))
        softmax(0, 1)
        weighted_values(0, block_at(2 * t + 1))
        return c
    lax.fori_loop(0, n_visit // 2, body, 0)

    for hh in range(hpb):
        o_t = acc_ref[hh, 0:HEAD_DIM, :] / acc_ref[hh, HEAD_DIM:HEAD_DIM + 1, :]
        o_ref[:, hh * HEAD_DIM:(hh + 1) * HEAD_DIM] = o_t.T.astype(o_ref.dtype)


def _attention(proj3, kmean3, slopes):
    B, S, _ = proj3.shape
    nb = S // MOBA_BLOCK
    hpb = ATTN_HEADS_PER_STEP
    tq = 2 * MOBA_BLOCK
    assert S % tq == 0 and nb % 16 == 0 and nb + 2 <= LANES and N_HEADS % hpb == 0
    A = N_HEADS * HEAD_DIM
    w = hpb * HEAD_DIM
    n_hg = N_HEADS // hpb
    grid_spec = pltpu.PrefetchScalarGridSpec(
        num_scalar_prefetch=1,
        grid=(B, n_hg, S // tq),
        in_specs=[
            pl.BlockSpec((None, tq, w), lambda b, h, i, s: (b, i, h)),
            pl.BlockSpec((None, S, w), lambda b, h, i, s: (b, 0, n_hg + h)),
            pl.BlockSpec((None, S, w), lambda b, h, i, s: (b, 0, 2 * n_hg + h)),
            pl.BlockSpec((None, nb, w), lambda b, h, i, s: (b, 0, h)),
        ],
        out_specs=pl.BlockSpec((None, tq, w), lambda b, h, i, s: (b, i, h)),
        scratch_shapes=[
            pltpu.VMEM((hpb, nb, MOBA_BLOCK, HEAD_DIM + LANES), BF16),
            pltpu.VMEM((hpb, nb, HEAD_DIM + V_PAD_ROWS, MOBA_BLOCK), BF16),
            pltpu.VMEM((hpb, HEAD_DIM + LANES, tq), BF16),
            pltpu.VMEM((2, hpb, MOBA_BLOCK, tq), F32),
            pltpu.VMEM((2, hpb, 1, tq), F32),
            pltpu.VMEM((2, hpb, MOBA_BLOCK, tq), BF16),
            pltpu.VMEM((2, hpb, 1, tq), F32),
            pltpu.VMEM((hpb, 1, tq), F32),
            pltpu.VMEM((hpb, HEAD_DIM + V_PAD_ROWS, tq), F32),
        ],
    )
    return pl.pallas_call(
        functools.partial(_attn_kernel, nb=nb, hpb=hpb, tq=tq),
        grid_spec=grid_spec,
        out_shape=jax.ShapeDtypeStruct((B, S, A), BF16),
        compiler_params=pltpu.CompilerParams(
            dimension_semantics=("arbitrary", "arbitrary", "arbitrary"),
            vmem_limit_bytes=VMEM_LIMIT, flags=ATTN_FLAGS),
        name="moba_attn",
    )(slopes, proj3, proj3, proj3, kmean3)


def _mix_kernel(o_ref, u_ref, halo_ref, band_ref, wpool_ref, beta_ref, pscale_ref, wout_ref,
                x_ref, gffn_ref, wr_hi_ref, wr_lo_ref, br_ref,
                h1_ref, f_ref, eid_ref, wts_ref, *, tm, seq):
    i = pl.program_id(0)
    t0 = (i * tm) % seq
    u = u_ref[...]
    halo = jnp.where(t0 == 0, jnp.zeros_like(halo_ref[...]), halo_ref[...])
    uext = jnp.concatenate([halo, u], axis=0)
    t = t0 + lax.broadcasted_iota(jnp.int32, (tm, 1), 0)
    zs = []
    for g, w in enumerate(POOL_WINDOWS):
        cs = slice(g * POOL_GROUP, (g + 1) * POOL_GROUP)
        wsum = jnp.dot(band_ref[g], uext[:, cs], preferred_element_type=F32)
        cnt = jnp.minimum(t + 1, w).astype(F32)
        z = wsum / cnt - u[:, cs].astype(F32)
        zs.append(jnp.dot(z.astype(BF16), wpool_ref[g], preferred_element_type=F32))
    o_pool = jnp.concatenate(zs, axis=1)
    mixed = jnp.concatenate(
        [_rms(o_ref[...].astype(F32), beta_ref[...]).astype(BF16),
         _rms(o_pool, pscale_ref[...]).astype(BF16)], axis=1)
    h1 = x_ref[...] + jnp.dot(mixed, wout_ref[...], preferred_element_type=F32)
    h1_ref[...] = h1
    f = _rms(h1, gffn_ref[...])
    f_ref[...] = f

    f_hi = f.astype(BF16)
    f_lo = (f - f_hi.astype(F32)).astype(BF16)
    wr_hi = wr_hi_ref[...]
    logits = (jnp.dot(f_hi, wr_hi, preferred_element_type=F32)
              + jnp.dot(f_lo, wr_hi, preferred_element_type=F32)
              + jnp.dot(f_hi, wr_lo_ref[...], preferred_element_type=F32))
    lt = logits.T + br_ref[...]
    row8 = lax.broadcasted_iota(jnp.int32, (N_GROUPS, tm), 0)
    gl = lt[0:N_GROUPS]
    gmax = jnp.max(gl, axis=0, keepdims=True)
    gidx = jnp.min(jnp.where(gl == gmax, row8, N_GROUPS), axis=0, keepdims=True)
    g_val = 1.0 / jnp.sum(jnp.exp(gl - gmax), axis=0, keepdims=True)
    el = jnp.zeros((EXPERTS_PER_GROUP, tm), F32)
    for g in range(N_GROUPS):
        lo = N_GROUPS + g * EXPERTS_PER_GROUP
        el = jnp.where(gidx == g, lt[lo:lo + EXPERTS_PER_GROUP], el)
    emax = jnp.max(el, axis=0, keepdims=True)
    esum = jnp.sum(jnp.exp(el - emax), axis=0, keepdims=True)
    i1 = jnp.min(jnp.where(el == emax, row8, EXPERTS_PER_GROUP), axis=0, keepdims=True)
    el2 = jnp.where(row8 == i1, -jnp.inf, el)
    m2 = jnp.max(el2, axis=0, keepdims=True)
    i2 = jnp.min(jnp.where(el2 == m2, row8, EXPERTS_PER_GROUP), axis=0, keepdims=True)
    p1 = 1.0 / esum
    p2 = jnp.exp(m2 - emax) / esum
    den = p1 + p2
    eid_ref[0:1, :] = gidx * EXPERTS_PER_GROUP + i1
    eid_ref[1:2, :] = gidx * EXPERTS_PER_GROUP + i2
    wts_ref[0:1, :] = g_val * (p1 / den)
    wts_ref[1:2, :] = g_val * (p2 / den)


def _pool_band(tm):
    r = jnp.arange(tm)[:, None]
    c = jnp.arange(POOL_HALO + tm)[None, :] - POOL_HALO
    return jnp.stack([((c <= r) & (c >= r - w + 1)).astype(BF16) for w in POOL_WINDOWS])


def _mix(o2, proj, x2, seq, w_pool, beta_attn, pool_scale, w_out, g_ffn, w_rg, b_rg, w_re, b_re):
    T, D = x2.shape
    A = N_HEADS * HEAD_DIM
    PW = len(POOL_WINDOWS) * POOL_GROUP
    tm = MIX_TM
    assert seq % tm == 0 and tm % POOL_HALO == 0 and POOL_HALO >= max(POOL_WINDOWS) - 1
    u_col = (proj.shape[1] - PW) // PW
    halo_per_tile = tm // POOL_HALO
    wr = jnp.concatenate([w_rg, jnp.transpose(w_re, (1, 0, 2)).reshape(D, N_EXPERTS)], axis=1)
    wr = jnp.pad(wr, ((0, 0), (0, LANES - wr.shape[1])))
    wr_hi = wr.astype(BF16)
    wr_lo = (wr - wr_hi.astype(F32)).astype(BF16)
    br = jnp.pad(jnp.concatenate([b_rg, b_re.reshape(-1)]), (0, LANES - N_GROUPS - N_EXPERTS))
    const = lambda shape: pl.BlockSpec(shape, lambda i: (0,) * len(shape))
    return pl.pallas_call(
        functools.partial(_mix_kernel, tm=tm, seq=seq),
        grid=(T // tm,),
        in_specs=[
            pl.BlockSpec((tm, A), lambda i: (i, 0)),
            pl.BlockSpec((tm, PW), lambda i: (i, u_col)),
            pl.BlockSpec((POOL_HALO, PW),
                         lambda i: (jnp.maximum(i * halo_per_tile - 1, 0), u_col)),
            const((len(POOL_WINDOWS), tm, POOL_HALO + tm)),
            const((len(POOL_WINDOWS), POOL_GROUP, POOL_GROUP)),
            const((1, A)), const((1, PW)), const((A + PW, D)),
            pl.BlockSpec((tm, D), lambda i: (i, 0)),
            const((1, D)), const((D, LANES)), const((D, LANES)), const((LANES, 1)),
        ],
        out_specs=[
            pl.BlockSpec((tm, D), lambda i: (i, 0)),
            pl.BlockSpec((tm, D), lambda i: (i, 0)),
            pl.BlockSpec((2, tm), lambda i: (0, i)),
            pl.BlockSpec((2, tm), lambda i: (0, i)),
        ],
        out_shape=[
            jax.ShapeDtypeStruct((T, D), F32),
            jax.ShapeDtypeStruct((T, D), F32),
            jax.ShapeDtypeStruct((2, T), jnp.int32),
            jax.ShapeDtypeStruct((2, T), F32),
        ],
        compiler_params=pltpu.CompilerParams(
            dimension_semantics=("arbitrary",), vmem_limit_bytes=VMEM_LIMIT),
        name="mix_router",
    )(o2, proj, proj, _pool_band(tm), w_pool.astype(BF16), beta_attn.reshape(1, A),
      pool_scale.reshape(1, PW), w_out.astype(BF16), x2, g_ffn.reshape(1, D),
      wr_hi, wr_lo, br.reshape(LANES, 1))


def _expert_kernel(blk_e_ref, n_used_ref, tok_ref, tok_next_ref, slot_ref, w_ref,
                   wg_ref, wu_ref, wd_ref, f_hbm, y_hbm,
                   xbuf, ybuf, wg_s, wu_s, wd_s, gsem, ssem, *, mb, n_rows):
    b = pl.program_id(0)
    n_used = n_used_ref[0]
    cur = b % 2

    def gather_copy(tok, r, slot):
        return pltpu.make_async_copy(f_hbm.at[pl.ds(tok, 1), :],
                                     xbuf.at[slot, pl.ds(r, 1), :], gsem.at[slot])

    def scatter_copy(dst, r, slot):
        return pltpu.make_async_copy(ybuf.at[slot, pl.ds(r, 1), :],
                                     y_hbm.at[pl.ds(dst, 1), :], ssem.at[slot])

    def start_gather(idx_ref, slot):
        def body(r, c):
            gather_copy(idx_ref[0, 0, r], r, slot).start()
            return c
        lax.fori_loop(0, mb, body, 0, unroll=8)

    def wait_rows(copy_fn, slot):
        def body(r, c):
            copy_fn(0, r, slot).wait()
            return c
        lax.fori_loop(0, mb, body, 0, unroll=8)

    @pl.when(b < n_used)
    def _active():
        @pl.when(b == 0)
        def _():
            start_gather(tok_ref, 0)
            ybuf[1] = jnp.zeros(ybuf.shape[1:], ybuf.dtype)
            for half in range(2):
                fill = pltpu.make_async_copy(
                    ybuf.at[1], y_hbm.at[pl.ds(n_rows + half * mb, mb), :], ssem.at[1])
                fill.start()
                fill.wait()

        @pl.when(b + 1 < n_used)
        def _():
            start_gather(tok_next_ref, 1 - cur)

        first_of_expert = jnp.logical_or(b == 0, blk_e_ref[b] != blk_e_ref[jnp.maximum(b - 1, 0)])

        @pl.when(first_of_expert)
        def _():
            wg_s[...] = wg_ref[...].astype(BF16)
            wu_s[...] = wu_ref[...].astype(BF16)
            wd_s[...] = wd_ref[...].astype(BF16)

        wait_rows(gather_copy, cur)

        @pl.when(b >= 2)
        def _():
            wait_rows(scatter_copy, cur)

        xb = xbuf[cur].astype(BF16)
        gate = jnp.dot(xb, wg_s[...], preferred_element_type=F32)
        up = jnp.dot(xb, wu_s[...], preferred_element_type=F32)
        hid = gate * jax.nn.sigmoid(gate) * up
        y = jnp.dot(hid.astype(BF16), wd_s[...], preferred_element_type=F32)
        ybuf[cur] = y * w_ref[...]

        def body(r, c):
            scatter_copy(slot_ref[0, 0, r], r, cur).start()
            return c
        lax.fori_loop(0, mb, body, 0, unroll=8)

        @pl.when(b == n_used - 1)
        def _drain():
            @pl.when(b >= 1)
            def _():
                wait_rows(scatter_copy, 1 - cur)
            wait_rows(scatter_copy, cur)


def _experts(f2, eid, wts, w_gate, w_up, w_down):
    T, D = f2.shape
    DE = w_gate.shape[-1]
    mb = EXPERT_MB
    N = 2 * T
    flat_e = eid.T.reshape(N)
    flat_w = wts.T.reshape(N)
    order = jnp.argsort(flat_e, stable=True).astype(jnp.int32)
    experts = jnp.arange(N_EXPERTS, dtype=jnp.int32)
    counts = jnp.sum((flat_e[:, None] == experts[None, :]).astype(jnp.int32), axis=0)
    starts = jnp.cumsum(counts) - counts
    padded = (counts + mb - 1) // mb * mb
    pend = jnp.cumsum(padded)
    pstarts = pend - padded
    P = N + N_EXPERTS * mb
    nblk = P // mb
    n_used = (pend[-1] // mb).astype(jnp.int32)
    blk_start = jnp.arange(nblk, dtype=jnp.int32) * mb
    blk_e = jnp.sum((pend[None, :] <= blk_start[:, None]).astype(jnp.int32), axis=1)
    last_e = jnp.max(jnp.where(counts > 0, experts, 0))
    blk_e = jnp.minimum(blk_e, last_e)
    r = jnp.arange(mb, dtype=jnp.int32)[None, :]
    off = (blk_start - pstarts[blk_e])[:, None] + r
    valid = off < counts[blk_e][:, None]
    slot = order[jnp.where(valid, starts[blk_e][:, None] + off, 0)]
    spare_row = N + (jnp.arange(nblk, dtype=jnp.int32)[:, None] % 2) * mb + r
    tok_buf = jnp.where(valid, slot // 2, 0)
    slot_buf = jnp.where(valid, (slot % 2) * T + slot // 2, spare_row)
    w_buf = jnp.where(valid, flat_w[slot], 0.0)

    grid_spec = pltpu.PrefetchScalarGridSpec(
        num_scalar_prefetch=2,
        grid=(nblk,),
        in_specs=[
            pl.BlockSpec((1, 1, mb), lambda b, be, nu: (b, 0, 0), memory_space=pltpu.SMEM),
            pl.BlockSpec((1, 1, mb), lambda b, be, nu: (jnp.minimum(b + 1, nblk - 1), 0, 0),
                         memory_space=pltpu.SMEM),
            pl.BlockSpec((1, 1, mb), lambda b, be, nu: (b, 0, 0), memory_space=pltpu.SMEM),
            pl.BlockSpec((mb, 1), lambda b, be, nu: (b, 0)),
            pl.BlockSpec((None, D, DE), lambda b, be, nu: (be[b], 0, 0)),
            pl.BlockSpec((None, D, DE), lambda b, be, nu: (be[b], 0, 0)),
            pl.BlockSpec((None, DE, D), lambda b, be, nu: (be[b], 0, 0)),
            pl.BlockSpec(memory_space=pl.ANY),
        ],
        out_specs=pl.BlockSpec(memory_space=pl.ANY),
        scratch_shapes=[
            pltpu.VMEM((2, mb, D), F32),
            pltpu.VMEM((2, mb, D), F32),
            pltpu.VMEM((D, DE), BF16), pltpu.VMEM((D, DE), BF16), pltpu.VMEM((DE, D), BF16),
            pltpu.SemaphoreType.DMA((2,)), pltpu.SemaphoreType.DMA((2,)),
        ],
    )
    y = pl.pallas_call(
        functools.partial(_expert_kernel, mb=mb, n_rows=N),
        grid_spec=grid_spec,
        out_shape=jax.ShapeDtypeStruct((N + 2 * mb, D), F32),
        compiler_params=pltpu.CompilerParams(
            dimension_semantics=("arbitrary",), vmem_limit_bytes=VMEM_LIMIT),
        name="experts",
    )(blk_e, n_used.reshape(1), tok_buf.reshape(nblk, 1, mb), tok_buf.reshape(nblk, 1, mb),
      slot_buf.reshape(nblk, 1, mb), w_buf.reshape(P, 1), w_gate, w_up, w_down, f2)
    return y


def _ple_kernel(h1_ref, y0_ref, y1_ref, p_ref, gple_ref, wg_ref, bg_ref, wple_ref, gfin_ref, out_ref,
                *, final):
    h = h1_ref[...] + y0_ref[...] + y1_ref[...]
    a = _rms(h, gple_ref[...]).astype(BF16)
    gate = jax.nn.sigmoid(jnp.dot(a, wg_ref[...], preferred_element_type=F32) + bg_ref[...])
    ple = jnp.dot(p_ref[...].astype(BF16), wple_ref[...], preferred_element_type=F32)
    h = h + gate * ple
    out_ref[...] = _rms(h, gfin_ref[...]) if final else h


def _ple(h1, y3, p2, g_ple, w_ple_gate, b_ple_gate, w_ple, g_final, final):
    T, D = h1.shape
    PD = p2.shape[1]
    tm = PLE_TM
    const = lambda shape: pl.BlockSpec(shape, lambda i: (0,) * len(shape))
    return pl.pallas_call(
        functools.partial(_ple_kernel, final=final),
        grid=(T // tm,),
        in_specs=[
            pl.BlockSpec((tm, D), lambda i: (i, 0)),
            pl.BlockSpec((tm, D), lambda i: (i, 0)),
            pl.BlockSpec((tm, D), lambda i: (T // tm + i, 0)),
            pl.BlockSpec((tm, PD), lambda i: (i, 0)),
            const((1, D)), const((D, D)), const((1, D)), const((PD, D)), const((1, D)),
        ],
        out_specs=pl.BlockSpec((tm, D), lambda i: (i, 0)),
        out_shape=jax.ShapeDtypeStruct((T, D), F32),
        compiler_params=pltpu.CompilerParams(
            dimension_semantics=("arbitrary",), vmem_limit_bytes=VMEM_LIMIT),
        name="ple_final",
    )(h1, y3, y3, p2, g_ple.reshape(1, D), w_ple_gate.astype(BF16), b_ple_gate.reshape(1, D),
      w_ple.astype(BF16), g_final.reshape(1, D))


def kernel(x, p, g_mix, w_in, beta_attn, w_pool, pool_scale, w_out, g_ffn, w_router_group,
           b_router_group, w_router_expert, b_router_expert, w_expert_gate, w_expert_up,
           w_expert_down, g_ple, w_ple, w_ple_gate, b_ple_gate, g_final):
    B, S, D = x.shape
    depth = w_in.shape[0]
    T = B * S
    A = N_HEADS * HEAD_DIM
    slopes = jnp.asarray([2.0 ** (-8.0 * (h + 1) / N_HEADS) for h in range(N_HEADS)], F32)
    h = x.reshape(T, D)
    for i in range(depth):
        proj, kmean = _inproj(h, g_mix[i], w_in[i].astype(BF16), HEAD_DIM ** -0.5)
        o = _attention(proj.reshape(B, S, -1), kmean.reshape(B, S // MOBA_BLOCK, A), slopes)
        h1, f, eid, wts = _mix(o.reshape(T, A), proj, h, S, w_pool[i], beta_attn[i], pool_scale[i],
                               w_out[i], g_ffn[i], w_router_group[i], b_router_group[i],
                               w_router_expert[i], b_router_expert[i])
        y3 = _experts(f, eid, wts, w_expert_gate[i], w_expert_up[i], w_expert_down[i])
        h = _ple(h1, y3, p[i].reshape(T, -1), g_ple[i], w_ple_gate[i], b_ple_gate[i], w_ple[i],
                 g_final, final=(i == depth - 1))
    return h.reshape(B, S, D)
```

```python
import functools

import jax
import jax.numpy as jnp
from jax import lax
from jax.experimental import pallas as pl
from jax.experimental.pallas import tpu as pltpu

F32 = jnp.float32
BF16 = jnp.bfloat16

N_HEADS = 8
HEAD_DIM = 128
MOBA_BLOCK = 256
MOBA_TOPK = 3
POOL_WINDOWS = (2, 4, 8, 16)
POOL_GROUP = 256
N_GROUPS = 8
EXPERTS_PER_GROUP = 8
N_EXPERTS = N_GROUPS * EXPERTS_PER_GROUP
EPS = 1e-6

LANES = 128
NEG = -1e30
VMEM_LIMIT = 56 * 1024 * 1024

INPROJ_TM = 512
MIX_TM = 256
POOL_HALO = 128
EXPERT_MB = 256
EXPERT_CHUNK = 256
EXPERT_EARLY_SCATTER = 64
PLE_TM = 256
ATTN_HEADS_PER_STEP = 2
V_PAD_ROWS = 16
ATTN_FLAGS = None

_NT = (((1,), (1,)), ((), ()))


def _rms(x, gain):
    return x * lax.rsqrt(jnp.mean(x * x, axis=-1, keepdims=True) + EPS) * gain


def _inproj_kernel(x_ref, g_ref, w_ref, proj_ref, kmean_ref, a_ref, *, tm, scale):
    j = pl.program_id(1)

    @pl.when(j == 0)
    def _():
        a_ref[...] = _rms(x_ref[...], g_ref[...]).astype(BF16)

    acc = jnp.dot(a_ref[...], w_ref[...], preferred_element_type=F32)
    proj_ref[...] = (acc * jnp.where(j == 0, scale, 1.0)).astype(BF16)

    @pl.when(j == 1)
    def _():
        kmean_ref[0] = jnp.mean(acc.reshape(tm // MOBA_BLOCK, MOBA_BLOCK, acc.shape[-1]), axis=1)


def _inproj(x2, g_mix, w_in_bf16, scale):
    T, D = x2.shape
    A = N_HEADS * HEAD_DIM
    tm = INPROJ_TM
    n_chunks = w_in_bf16.shape[1] // A
    return pl.pallas_call(
        functools.partial(_inproj_kernel, tm=tm, scale=scale),
        grid=(T // tm, n_chunks),
        in_specs=[
            pl.BlockSpec((tm, D), lambda i, j: (i, 0)),
            pl.BlockSpec((1, D), lambda i, j: (0, 0)),
            pl.BlockSpec((D, A), lambda i, j: (0, j)),
        ],
        out_specs=[
            pl.BlockSpec((tm, A), lambda i, j: (i, j)),
            pl.BlockSpec((1, tm // MOBA_BLOCK, A), lambda i, j: (i, 0, 0)),
        ],
        out_shape=[
            jax.ShapeDtypeStruct((T, n_chunks * A), BF16),
            jax.ShapeDtypeStruct((T // tm, tm // MOBA_BLOCK, A), F32),
        ],
        scratch_shapes=[pltpu.VMEM((tm, D), BF16)],
        compiler_params=pltpu.CompilerParams(
            dimension_semantics=("arbitrary", "arbitrary"), vmem_limit_bytes=VMEM_LIMIT),
        name="inproj",
    )(x2, g_mix.reshape(1, D), w_in_bf16)


def _attn_kernel(slopes_ref, q_ref, k_ref, v_ref, km_ref, o_ref,
                 kaug_ref, vt_ref, qaug_ref, s_ref, bmax_ref, p_ref, alpha_ref, m_ref, acc_ref,
                 *, nb, hpb, tq):
    hg = pl.program_id(1)
    i = pl.program_id(2)
    blk_rows = MOBA_BLOCK
    n_feat = LANES

    @pl.when(i == 0)
    def _build_kv():
        for hh in range(hpb):
            slope = slopes_ref[hg * hpb + hh]
            cs = slice(hh * HEAD_DIM, (hh + 1) * HEAD_DIM)

            def body(j, c, hh=hh, slope=slope, cs=cs):
                r0 = pl.multiple_of(j * blk_rows, blk_rows)
                kaug_ref[hh, j, :, 0:HEAD_DIM] = k_ref[pl.ds(r0, blk_rows), cs]
                lane = lax.broadcasted_iota(jnp.int32, (blk_rows, n_feat), 1)
                kj = lax.broadcasted_iota(jnp.int32, (blk_rows, n_feat), 0).astype(F32)
                feat = jnp.where(lane == j, 1.0,
                                 jnp.where(lane == nb, 1.0,
                                           jnp.where(lane == nb + 1, slope * kj, 0.0)))
                kaug_ref[hh, j, :, HEAD_DIM:HEAD_DIM + n_feat] = feat.astype(BF16)
                vt_ref[hh, j, 0:HEAD_DIM, :] = v_ref[pl.ds(r0, blk_rows), cs].astype(F32).T.astype(BF16)
                ones_row = lax.broadcasted_iota(jnp.int32, (V_PAD_ROWS, blk_rows), 0) == 0
                vt_ref[hh, j, HEAD_DIM:HEAD_DIM + V_PAD_ROWS, :] = jnp.where(ones_row, 1.0, 0.0).astype(BF16)
                return c
            lax.fori_loop(0, nb, body, 0)

    blk = lax.broadcasted_iota(jnp.int32, (nb, tq), 0)
    own = 2 * i + lax.broadcasted_iota(jnp.int32, (nb, tq), 1) // blk_rows
    row2 = lax.broadcasted_iota(jnp.int32, (n_feat - nb, tq), 0)
    qi = (lax.broadcasted_iota(jnp.int32, (n_feat - nb, tq), 1) % blk_rows).astype(F32)
    for hh in range(hpb):
        slope = slopes_ref[hg * hpb + hh]
        cs = slice(hh * HEAD_DIM, (hh + 1) * HEAD_DIM)
        q_t = q_ref[:, cs].astype(F32).T.astype(BF16)
        km = km_ref[:, cs]
        km_hi = km.astype(BF16)
        km_lo = (km - km_hi.astype(F32)).astype(BF16)
        g = (jnp.dot(km_hi, q_t, preferred_element_type=F32)
             + jnp.dot(km_lo, q_t, preferred_element_type=F32))
        g = jnp.where(blk < own, g, -jnp.inf)
        sel = blk == own
        for r in range(min(MOBA_TOPK, nb)):
            mx = jnp.max(g, axis=0, keepdims=True)
            idx = jnp.min(jnp.where(g == mx, blk, nb), axis=0, keepdims=True)
            pick = jnp.logical_and(blk == idx, r < own)
            sel = jnp.logical_or(sel, pick)
            g = jnp.where(pick, -jnp.inf, g)
        bias = jnp.where(sel, (-slope * blk_rows) * (own - blk).astype(F32), NEG)
        extra = jnp.where(row2 == 0, -slope * qi, jnp.where(row2 == 1, 1.0, 0.0))
        qaug_ref[hh, 0:HEAD_DIM, :] = q_t
        qaug_ref[hh, HEAD_DIM:HEAD_DIM + nb, :] = bias.astype(BF16)
        qaug_ref[hh, HEAD_DIM + nb:HEAD_DIM + n_feat, :] = extra.astype(BF16)

    n_visit = 2 * i + 2

    def block_at(t):
        return jnp.where(t < 2, 2 * i + t, jnp.minimum(t - 2, nb - 1))

    def scores(slot, b, diag=None):
        for hh in range(hpb):
            s = jnp.dot(kaug_ref[hh, b], qaug_ref[hh], preferred_element_type=F32)
            if diag is not None:
                key_pos = lax.broadcasted_iota(jnp.int32, (blk_rows, tq), 0) + diag * blk_rows
                qry_pos = lax.broadcasted_iota(jnp.int32, (blk_rows, tq), 1)
                s = jnp.where(key_pos <= qry_pos, s, NEG)
            s_ref[slot, hh] = s
            bmax_ref[slot, hh] = jnp.max(s, axis=0, keepdims=True)

    def softmax(sslot, pslot):
        for hh in range(hpb):
            m_prev = m_ref[hh]
            m_new = jnp.maximum(m_prev, bmax_ref[sslot, hh])
            alpha_ref[pslot, hh] = jnp.exp(m_prev - m_new)
            p_ref[pslot, hh] = jnp.exp(s_ref[sslot, hh] - m_new).astype(BF16)
            m_ref[hh] = m_new

    def weighted_values(pslot, b):
        for hh in range(hpb):
            acc_ref[hh] = alpha_ref[pslot, hh] * acc_ref[hh] + jnp.dot(
                vt_ref[hh, b], p_ref[pslot, hh], preferred_element_type=F32)

    m_ref[...] = jnp.full(m_ref.shape, -jnp.inf, F32)
    acc_ref[...] = jnp.zeros(acc_ref.shape, F32)
    scores(0, 2 * i, diag=0)
    scores(1, 2 * i + 1, diag=1)
    softmax(0, 1)

    def body(t, c):
        scores(0, block_at(2 * t + 2))
        softmax(1, 0)
        weighted_values(1, block_at(2 * t))
        scores(1, block_at(2 * t + 3))
        softmax(0, 1)
        weighted_values(0, block_at(2 * t + 1))
        return c
    lax.fori_loop(0, n_visit // 2, body, 0)

    for hh in range(hpb):
        o_t = acc_ref[hh, 0:HEAD_DIM, :] / acc_ref[hh, HEAD_DIM:HEAD_DIM + 1, :]
        o_ref[:, hh * HEAD_DIM:(hh + 1) * HEAD_DIM] = o_t.T.astype(o_ref.dtype)


def _attention(proj3, kmean3, slopes):
    B, S, _ = proj3.shape
    nb = S // MOBA_BLOCK
    hpb = ATTN_HEADS_PER_STEP
    tq = 2 * MOBA_BLOCK
    assert S % tq == 0 and nb % 16 == 0 and nb + 2 <= LANES and N_HEADS % hpb == 0
    A = N_HEADS * HEAD_DIM
    w = hpb * HEAD_DIM
    n_hg = N_HEADS // hpb
    grid_spec = pltpu.PrefetchScalarGridSpec(
        num_scalar_prefetch=1,
        grid=(B, n_hg, S // tq),
        in_specs=[
            pl.BlockSpec((None, tq, w), lambda b, h, i, s: (b, i, h)),
            pl.BlockSpec((None, S, w), lambda b, h, i, s: (b, 0, n_hg + h)),
            pl.BlockSpec((None, S, w), lambda b, h, i, s: (b, 0, 2 * n_hg + h)),
            pl.BlockSpec((None, nb, w), lambda b, h, i, s: (b, 0, h)),
        ],
        out_specs=pl.BlockSpec((None, tq, w), lambda b, h, i, s: (b, i, h)),
        scratch_shapes=[
            pltpu.VMEM((hpb, nb, MOBA_BLOCK, HEAD_DIM + LANES), BF16),
            pltpu.VMEM((hpb, nb, HEAD_DIM + V_PAD_ROWS, MOBA_BLOCK), BF16),
            pltpu.VMEM((hpb, HEAD_DIM + LANES, tq), BF16),
            pltpu.VMEM((2, hpb, MOBA_BLOCK, tq), F32),
            pltpu.VMEM((2, hpb, 1, tq), F32),
            pltpu.VMEM((2, hpb, MOBA_BLOCK, tq), BF16),
            pltpu.VMEM((2, hpb, 1, tq), F32),
            pltpu.VMEM((hpb, 1, tq), F32),
            pltpu.VMEM((hpb, HEAD_DIM + V_PAD_ROWS, tq), F32),
        ],
    )
    return pl.pallas_call(
        functools.partial(_attn_kernel, nb=nb, hpb=hpb, tq=tq),
        grid_spec=grid_spec,
        out_shape=jax.ShapeDtypeStruct((B, S, A), BF16),
        compiler_params=pltpu.CompilerParams(
            dimension_semantics=("arbitrary", "arbitrary", "arbitrary"),
            vmem_limit_bytes=VMEM_LIMIT, flags=ATTN_FLAGS),
        name="moba_attn",
    )(slopes, proj3, proj3, proj3, kmean3)


def _mix_kernel(o_ref, u_ref, halo_ref, band_ref, wpool_ref, beta_ref, pscale_ref, wout_ref,
                x_ref, gffn_ref, wr_hi_ref, wr_lo_ref, br_ref,
                h1_ref, f_ref, eid_ref, wts_ref, *, tm, seq):
    i = pl.program_id(0)
    t0 = (i * tm) % seq
    u = u_ref[...]
    halo = jnp.where(t0 == 0, jnp.zeros_like(halo_ref[...]), halo_ref[...])
    uext = jnp.concatenate([halo, u], axis=0)
    t = t0 + lax.broadcasted_iota(jnp.int32, (tm, 1), 0)
    zs = []
    for g, w in enumerate(POOL_WINDOWS):
        cs = slice(g * POOL_GROUP, (g + 1) * POOL_GROUP)
        wsum = jnp.dot(band_ref[g], uext[:, cs], preferred_element_type=F32)
        cnt = jnp.minimum(t + 1, w).astype(F32)
        z = wsum / cnt - u[:, cs].astype(F32)
        zs.append(jnp.dot(z.astype(BF16), wpool_ref[g], preferred_element_type=F32))
    o_pool = jnp.concatenate(zs, axis=1)
    mixed = jnp.concatenate(
        [_rms(o_ref[...].astype(F32), beta_ref[...]).astype(BF16),
         _rms(o_pool, pscale_ref[...]).astype(BF16)], axis=1)
    h1 = x_ref[...] + jnp.dot(mixed, wout_ref[...], preferred_element_type=F32)
    h1_ref[...] = h1
    f = _rms(h1, gffn_ref[...])
    f_ref[...] = f

    f_hi = f.astype(BF16)
    f_lo = (f - f_hi.astype(F32)).astype(BF16)
    wr_hi = wr_hi_ref[...]
    logits = (jnp.dot(f_hi, wr_hi, preferred_element_type=F32)
              + jnp.dot(f_lo, wr_hi, preferred_element_type=F32)
              + jnp.dot(f_hi, wr_lo_ref[...], preferred_element_type=F32))
    lt = logits.T + br_ref[...]
    row8 = lax.broadcasted_iota(jnp.int32, (N_GROUPS, tm), 0)
    gl = lt[0:N_GROUPS]
    gmax = jnp.max(gl, axis=0, keepdims=True)
    gidx = jnp.min(jnp.where(gl == gmax, row8, N_GROUPS), axis=0, keepdims=True)
    g_val = 1.0 / jnp.sum(jnp.exp(gl - gmax), axis=0, keepdims=True)
    el = jnp.zeros((EXPERTS_PER_GROUP, tm), F32)
    for g in range(N_GROUPS):
        lo = N_GROUPS + g * EXPERTS_PER_GROUP
        el = jnp.where(gidx == g, lt[lo:lo + EXPERTS_PER_GROUP], el)
    emax = jnp.max(el, axis=0, keepdims=True)
    esum = jnp.sum(jnp.exp(el - emax), axis=0, keepdims=True)
    i1 = jnp.min(jnp.where(el == emax, row8, EXPERTS_PER_GROUP), axis=0, keepdims=True)
    el2 = jnp.where(row8 == i1, -jnp.inf, el)
    m2 = jnp.max(el2, axis=0, keepdims=True)
    i2 = jnp.min(jnp.where(el2 == m2, row8, EXPERTS_PER_GROUP), axis=0, keepdims=True)
    p1 = 1.0 / esum
    p2 = jnp.exp(m2 - emax) / esum
    den = p1 + p2
    eid_ref[0:1, :] = gidx * EXPERTS_PER_GROUP + i1
    eid_ref[1:2, :] = gidx * EXPERTS_PER_GROUP + i2
    wts_ref[0:1, :] = g_val * (p1 / den)
    wts_ref[1:2, :] = g_val * (p2 / den)


def _pool_band(tm):
    r = jnp.arange(tm)[:, None]
    c = jnp.arange(POOL_HALO + tm)[None, :] - POOL_HALO
    return jnp.stack([((c <= r) & (c >= r - w + 1)).astype(BF16) for w in POOL_WINDOWS])


def _mix(o2, proj, x2, seq, w_pool, beta_attn, pool_scale, w_out, g_ffn, w_rg, b_rg, w_re, b_re):
    T, D = x2.shape
    A = N_HEADS * HEAD_DIM
    PW = len(POOL_WINDOWS) * POOL_GROUP
    tm = MIX_TM
    assert seq % tm == 0 and tm % POOL_HALO == 0 and POOL_HALO >= max(POOL_WINDOWS) - 1
    u_col = (proj.shape[1] - PW) // PW
    halo_per_tile = tm // POOL_HALO
    wr = jnp.concatenate([w_rg, jnp.transpose(w_re, (1, 0, 2)).reshape(D, N_EXPERTS)], axis=1)
    wr = jnp.pad(wr, ((0, 0), (0, LANES - wr.shape[1])))
    wr_hi = wr.astype(BF16)
    wr_lo = (wr - wr_hi.astype(F32)).astype(BF16)
    br = jnp.pad(jnp.concatenate([b_rg, b_re.reshape(-1)]), (0, LANES - N_GROUPS - N_EXPERTS))
    const = lambda shape: pl.BlockSpec(shape, lambda i: (0,) * len(shape))
    return pl.pallas_call(
        functools.partial(_mix_kernel, tm=tm, seq=seq),
        grid=(T // tm,),
        in_specs=[
            pl.BlockSpec((tm, A), lambda i: (i, 0)),
            pl.BlockSpec((tm, PW), lambda i: (i, u_col)),
            pl.BlockSpec((POOL_HALO, PW),
                         lambda i: (jnp.maximum(i * halo_per_tile - 1, 0), u_col)),
            const((len(POOL_WINDOWS), tm, POOL_HALO + tm)),
            const((len(POOL_WINDOWS), POOL_GROUP, POOL_GROUP)),
            const((1, A)), const((1, PW)), const((A + PW, D)),
            pl.BlockSpec((tm, D), lambda i: (i, 0)),
            const((1, D)), const((D, LANES)), const((D, LANES)), const((LANES, 1)),
        ],
        out_specs=[
            pl.BlockSpec((tm, D), lambda i: (i, 0)),
            pl.BlockSpec((tm, D), lambda i: (i, 0)),
            pl.BlockSpec((2, tm), lambda i: (0, i)),
            pl.BlockSpec((2, tm), lambda i: (0, i)),
        ],
        out_shape=[
            jax.ShapeDtypeStruct((T, D), F32),
            jax.ShapeDtypeStruct((T, D), F32),
            jax.ShapeDtypeStruct((2, T), jnp.int32),
            jax.ShapeDtypeStruct((2, T), F32),
        ],
        compiler_params=pltpu.CompilerParams(
            dimension_semantics=("arbitrary",), vmem_limit_bytes=VMEM_LIMIT),
        name="mix_router",
    )(o2, proj, proj, _pool_band(tm), w_pool.astype(BF16), beta_attn.reshape(1, A),
      pool_scale.reshape(1, PW), w_out.astype(BF16), x2, g_ffn.reshape(1, D),
      wr_hi, wr_lo, br.reshape(LANES, 1))


def _expert_kernel(blk_e_ref, n_used_ref, tok_ref, tok_next_ref, slot_prev_ref, w_ref,
                   wg_ref, wu_ref, wd_ref, f_hbm, y_hbm,
                   xbuf, ybuf, wg_s, wu_s, wd_s, hid_s, gsem, ssem, *, mb, n_rows):
    b = pl.program_id(0)
    n_used = n_used_ref[0]
    cur = b % 2
    oth = 1 - cur
    d_model = xbuf.shape[-1]
    d_expert = wg_s.shape[-1]

    def gather_copy(tok, r, slot):
        return pltpu.make_async_copy(f_hbm.at[pl.ds(tok, 1), :],
                                     xbuf.at[slot, pl.ds(r, 1), :], gsem.at[slot])

    def scatter_copy(dst, r, slot):
        return pltpu.make_async_copy(ybuf.at[slot, pl.ds(r, 1), :],
                                     y_hbm.at[pl.ds(dst, 1), :], ssem.at[slot])

    def start_rows(copy_fn, idx_ref, slot):
        def body(r, c):
            copy_fn(idx_ref[0, 0, r], r, slot).start()
            return c
        lax.fori_loop(0, mb, body, 0, unroll=8)

    def wait_gather(slot):
        pltpu.make_async_copy(f_hbm.at[pl.ds(0, mb), :], xbuf.at[slot], gsem.at[slot]).wait()

    def wait_scatter(slot):
        pltpu.make_async_copy(ybuf.at[slot], y_hbm.at[pl.ds(0, mb), :], ssem.at[slot]).wait()

    @pl.when(b == 0)
    def _prologue():
        start_rows(gather_copy, tok_ref, 0)
        ybuf[1] = jnp.zeros(ybuf.shape[1:], ybuf.dtype)
        for half in range(2):
            fill = pltpu.make_async_copy(
                ybuf.at[1], y_hbm.at[pl.ds(n_rows + half * mb, mb), :], ssem.at[1])
            fill.start()
            fill.wait()

    @pl.when(b < n_used)
    def _block():
        first_of_expert = jnp.logical_or(b == 0, blk_e_ref[b] != blk_e_ref[jnp.maximum(b - 1, 0)])

        @pl.when(first_of_expert)
        def _():
            wg_s[...] = wg_ref[...].astype(BF16)
            wu_s[...] = wu_ref[...].astype(BF16)
            wd_s[...] = wd_ref[...].astype(BF16)

        wait_gather(cur)
        xb = xbuf[cur].astype(BF16)
        n_hid_chunks = d_expert // EXPERT_CHUNK
        rows_per = mb // n_hid_chunks
        early_per = EXPERT_EARLY_SCATTER // n_hid_chunks
        for c in range(n_hid_chunks):
            cs = slice(c * EXPERT_CHUNK, (c + 1) * EXPERT_CHUNK)
            gate = jnp.dot(xb, wg_s[:, cs], preferred_element_type=F32)
            up = jnp.dot(xb, wu_s[:, cs], preferred_element_type=F32)
            hid_s[:, cs] = (gate * jax.nn.sigmoid(gate) * up).astype(BF16)
            for r in range(c * rows_per, (c + 1) * rows_per):
                gather_copy(tok_next_ref[0, 0, r], r, oth).start()
            for r in range(c * early_per, (c + 1) * early_per):
                scatter_copy(slot_prev_ref[0, 0, r], r, oth).start()

        @pl.when(b >= 1)
        def _():
            wait_scatter(cur)

        hid = hid_s[...]
        n_out_chunks = d_model // EXPERT_CHUNK
        rows_per = (mb - EXPERT_EARLY_SCATTER) // n_out_chunks
        for c in range(n_out_chunks):
            cs = slice(c * EXPERT_CHUNK, (c + 1) * EXPERT_CHUNK)
            y = jnp.dot(hid, wd_s[:, cs], preferred_element_type=F32)
            ybuf[cur, :, cs] = y * w_ref[...]
            for r in range(EXPERT_EARLY_SCATTER + c * rows_per,
                           EXPERT_EARLY_SCATTER + (c + 1) * rows_per):
                scatter_copy(slot_prev_ref[0, 0, r], r, oth).start()

    @pl.when(b == n_used)
    def _flush():
        wait_gather(cur)
        wait_scatter(cur)
        start_rows(scatter_copy, slot_prev_ref, oth)
        wait_scatter(oth)


def _experts(f2, eid, wts, w_gate, w_up, w_down):
    T, D = f2.shape
    DE = w_gate.shape[-1]
    mb = EXPERT_MB
    N = 2 * T
    flat_e = eid.T.reshape(N)
    flat_w = wts.T.reshape(N)
    order = jnp.argsort(flat_e, stable=True).astype(jnp.int32)
    experts = jnp.arange(N_EXPERTS, dtype=jnp.int32)
    counts = jnp.sum((flat_e[:, None] == experts[None, :]).astype(jnp.int32), axis=0)
    starts = jnp.cumsum(counts) - counts
    padded = (counts + mb - 1) // mb * mb
    pend = jnp.cumsum(padded)
    pstarts = pend - padded
    P = N + N_EXPERTS * mb
    nblk = P // mb
    last = nblk - 1
    n_used = (pend[-1] // mb).astype(jnp.int32)
    blk_start = jnp.arange(nblk, dtype=jnp.int32) * mb
    blk_e = jnp.sum((pend[None, :] <= blk_start[:, None]).astype(jnp.int32), axis=1)
    last_e = jnp.max(jnp.where(counts > 0, experts, 0))
    blk_e = jnp.minimum(blk_e, last_e)
    r = jnp.arange(mb, dtype=jnp.int32)[None, :]
    off = (blk_start - pstarts[blk_e])[:, None] + r
    valid = off < counts[blk_e][:, None]
    slot = order[jnp.where(valid, starts[blk_e][:, None] + off, 0)]
    spare_row = N + (jnp.arange(nblk, dtype=jnp.int32)[:, None] % 2) * mb + r
    tok_buf = jnp.where(valid, slot // 2, 0)
    slot_buf = jnp.where(valid, (slot % 2) * T + slot // 2, spare_row)
    w_buf = jnp.where(valid, flat_w[slot], 0.0)
    slot_prev = jnp.concatenate([N + mb + r, slot_buf], axis=0)

    grid_spec = pltpu.PrefetchScalarGridSpec(
        num_scalar_prefetch=2,
        grid=(nblk + 1,),
        in_specs=[
            pl.BlockSpec((1, 1, mb), lambda b, be, nu: (jnp.minimum(b, last), 0, 0), memory_space=pltpu.SMEM),
            pl.BlockSpec((1, 1, mb), lambda b, be, nu: (jnp.minimum(b + 1, last), 0, 0),
                         memory_space=pltpu.SMEM),
            pl.BlockSpec((1, 1, mb), lambda b, be, nu: (b, 0, 0), memory_space=pltpu.SMEM),
            pl.BlockSpec((mb, 1), lambda b, be, nu: (jnp.minimum(b, last), 0)),
            pl.BlockSpec((None, D, DE), lambda b, be, nu: (be[jnp.minimum(b, last)], 0, 0)),
            pl.BlockSpec((None, D, DE), lambda b, be, nu: (be[jnp.minimum(b, last)], 0, 0)),
            pl.BlockSpec((None, DE, D), lambda b, be, nu: (be[jnp.minimum(b, last)], 0, 0)),
            pl.BlockSpec(memory_space=pl.ANY),
        ],
        out_specs=pl.BlockSpec(memory_space=pl.ANY),
        scratch_shapes=[
            pltpu.VMEM((2, mb, D), F32),
            pltpu.VMEM((2, mb, D), F32),
            pltpu.VMEM((D, DE), BF16), pltpu.VMEM((D, DE), BF16), pltpu.VMEM((DE, D), BF16),
            pltpu.VMEM((mb, DE), BF16),
            pltpu.SemaphoreType.DMA((2,)), pltpu.SemaphoreType.DMA((2,)),
        ],
    )
    y = pl.pallas_call(
        functools.partial(_expert_kernel, mb=mb, n_rows=N),
        grid_spec=grid_spec,
        out_shape=jax.ShapeDtypeStruct((N + 2 * mb, D), F32),
        compiler_params=pltpu.CompilerParams(
            dimension_semantics=("arbitrary",), vmem_limit_bytes=VMEM_LIMIT),
        name="experts",
    )(blk_e, n_used.reshape(1), tok_buf.reshape(nblk, 1, mb), tok_buf.reshape(nblk, 1, mb),
      slot_prev.reshape(nblk + 1, 1, mb), w_buf.reshape(P, 1), w_gate, w_up, w_down, f2)
    return y


def _ple_kernel(h1_ref, y0_ref, y1_ref, p_ref, gple_ref, wg_ref, bg_ref, wple_ref, gfin_ref, out_ref,
                *, final):
    h = h1_ref[...] + y0_ref[...] + y1_ref[...]
    a = _rms(h, gple_ref[...]).astype(BF16)
    gate = jax.nn.sigmoid(jnp.dot(a, wg_ref[...], preferred_element_type=F32) + bg_ref[...])
    ple = jnp.dot(p_ref[...].astype(BF16), wple_ref[...], preferred_element_type=F32)
    h = h + gate * ple
    out_ref[...] = _rms(h, gfin_ref[...]) if final else h


def _ple(h1, y3, p2, g_ple, w_ple_gate, b_ple_gate, w_ple, g_final, final):
    T, D = h1.shape
    PD = p2.shape[1]
    tm = PLE_TM
    const = lambda shape: pl.BlockSpec(shape, lambda i: (0,) * len(shape))
    return pl.pallas_call(
        functools.partial(_ple_kernel, final=final),
        grid=(T // tm,),
        in_specs=[
            pl.BlockSpec((tm, D), lambda i: (i, 0)),
            pl.BlockSpec((tm, D), lambda i: (i, 0)),
            pl.BlockSpec((tm, D), lambda i: (T // tm + i, 0)),
            pl.BlockSpec((tm, PD), lambda i: (i, 0)),
            const((1, D)), const((D, D)), const((1, D)), const((PD, D)), const((1, D)),
        ],
        out_specs=pl.BlockSpec((tm, D), lambda i: (i, 0)),
        out_shape=jax.ShapeDtypeStruct((T, D), F32),
        compiler_params=pltpu.CompilerParams(
            dimension_semantics=("arbitrary",), vmem_limit_bytes=VMEM_LIMIT),
        name="ple_final",
    )(h1, y3, y3, p2, g_ple.reshape(1, D), w_ple_gate.astype(BF16), b_ple_gate.reshape(1, D),
      w_ple.astype(BF16), g_final.reshape(1, D))


def kernel(x, p, g_mix, w_in, beta_attn, w_pool, pool_scale, w_out, g_ffn, w_router_group,
           b_router_group, w_router_expert, b_router_expert, w_expert_gate, w_expert_up,
           w_expert_down, g_ple, w_ple, w_ple_gate, b_ple_gate, g_final):
    B, S, D = x.shape
    depth = w_in.shape[0]
    T = B * S
    A = N_HEADS * HEAD_DIM
    slopes = jnp.asarray([2.0 ** (-8.0 * (h + 1) / N_HEADS) for h in range(N_HEADS)], F32)
    h = x.reshape(T, D)
    for i in range(depth):
        proj, kmean = _inproj(h, g_mix[i], w_in[i].astype(BF16), HEAD_DIM ** -0.5)
        o = _attention(proj.reshape(B, S, -1), kmean.reshape(B, S // MOBA_BLOCK, A), slopes)
        h1, f, eid, wts = _mix(o.reshape(T, A), proj, h, S, w_pool[i], beta_attn[i], pool_scale[i],
                               w_out[i], g_ffn[i], w_router_group[i], b_router_group[i],
                               w_router_expert[i], b_router_expert[i])
        y3 = _experts(f, eid, wts, w_expert_gate[i], w_expert_up[i], w_expert_down[i])
        h = _ple(h1, y3, p[i].reshape(T, -1), g_ple[i], w_ple_gate[i], b_ple_gate[i], w_ple[i],
                 g_final, final=(i == depth - 1))
    return h.reshape(B, S, D)
```

```python
import functools

import jax
import jax.numpy as jnp
from jax import lax
from jax.experimental import pallas as pl
from jax.experimental.pallas import tpu as pltpu

F32 = jnp.float32
BF16 = jnp.bfloat16

N_HEADS = 8
HEAD_DIM = 128
MOBA_BLOCK = 256
MOBA_TOPK = 3
POOL_WINDOWS = (2, 4, 8, 16)
POOL_GROUP = 256
N_GROUPS = 8
EXPERTS_PER_GROUP = 8
N_EXPERTS = N_GROUPS * EXPERTS_PER_GROUP
EPS = 1e-6

LANES = 128
NEG = -1e30
VMEM_LIMIT = 56 * 1024 * 1024

INPROJ_TM = 1024
MIX_TM = 256
POOL_HALO = 128
EXPERT_MB = 256
EXPERT_CHUNK = 256
EXPERT_EARLY_SCATTER = 64
PLE_TM = 256
ATTN_HEADS_PER_STEP = 2
V_PAD_ROWS = 16
ATTN_FLAGS = None

_NT = (((1,), (1,)), ((), ()))


def _rms(x, gain):
    return x * lax.rsqrt(jnp.mean(x * x, axis=-1, keepdims=True) + EPS) * gain


def _inproj_kernel(x_ref, g_ref, w_ref, proj_ref, kmean_ref, a_ref, *, tm, scale):
    j = pl.program_id(1)

    @pl.when(j == 0)
    def _():
        a_ref[...] = _rms(x_ref[...], g_ref[...]).astype(BF16)

    acc = jnp.dot(a_ref[...], w_ref[...], preferred_element_type=F32)
    proj_ref[...] = (acc * jnp.where(j == 0, scale, 1.0)).astype(BF16)

    @pl.when(j == 1)
    def _():
        kmean_ref[0] = jnp.mean(acc.reshape(tm // MOBA_BLOCK, MOBA_BLOCK, acc.shape[-1]), axis=1)


def _inproj(x2, g_mix, w_in_bf16, scale):
    T, D = x2.shape
    A = N_HEADS * HEAD_DIM
    tm = INPROJ_TM
    n_chunks = w_in_bf16.shape[1] // A
    return pl.pallas_call(
        functools.partial(_inproj_kernel, tm=tm, scale=scale),
        grid=(T // tm, n_chunks),
        in_specs=[
            pl.BlockSpec((tm, D), lambda i, j: (i, 0)),
            pl.BlockSpec((1, D), lambda i, j: (0, 0)),
            pl.BlockSpec((D, A), lambda i, j: (0, j)),
        ],
        out_specs=[
            pl.BlockSpec((tm, A), lambda i, j: (i, j)),
            pl.BlockSpec((1, tm // MOBA_BLOCK, A), lambda i, j: (i, 0, 0)),
        ],
        out_shape=[
            jax.ShapeDtypeStruct((T, n_chunks * A), BF16),
            jax.ShapeDtypeStruct((T // tm, tm // MOBA_BLOCK, A), F32),
        ],
        scratch_shapes=[pltpu.VMEM((tm, D), BF16)],
        compiler_params=pltpu.CompilerParams(
            dimension_semantics=("arbitrary", "arbitrary"), vmem_limit_bytes=VMEM_LIMIT),
        name="inproj",
    )(x2, g_mix.reshape(1, D), w_in_bf16)


def _attn_kernel(slopes_ref, q_ref, k_ref, v_ref, km_ref, o_ref,
                 kaug_ref, vt_ref, qaug_ref, s_ref, bmax_ref, p_ref, alpha_ref, m_ref, acc_ref,
                 *, nb, hpb, tq):
    hg = pl.program_id(1)
    i = pl.program_id(2)
    blk_rows = MOBA_BLOCK
    n_feat = LANES

    @pl.when(i == 0)
    def _build_kv():
        for hh in range(hpb):
            slope = slopes_ref[hg * hpb + hh]
            cs = slice(hh * HEAD_DIM, (hh + 1) * HEAD_DIM)

            def body(j, c, hh=hh, slope=slope, cs=cs):
                r0 = pl.multiple_of(j * blk_rows, blk_rows)
                kaug_ref[hh, j, :, 0:HEAD_DIM] = k_ref[pl.ds(r0, blk_rows), cs]
                lane = lax.broadcasted_iota(jnp.int32, (blk_rows, n_feat), 1)
                kj = lax.broadcasted_iota(jnp.int32, (blk_rows, n_feat), 0).astype(F32)
                feat = jnp.where(lane == j, 1.0,
                                 jnp.where(lane == nb, 1.0,
                                           jnp.where(lane == nb + 1, slope * kj, 0.0)))
                kaug_ref[hh, j, :, HEAD_DIM:HEAD_DIM + n_feat] = feat.astype(BF16)
                vt_ref[hh, j, 0:HEAD_DIM, :] = v_ref[pl.ds(r0, blk_rows), cs].astype(F32).T.astype(BF16)
                ones_row = lax.broadcasted_iota(jnp.int32, (V_PAD_ROWS, blk_rows), 0) == 0
                vt_ref[hh, j, HEAD_DIM:HEAD_DIM + V_PAD_ROWS, :] = jnp.where(ones_row, 1.0, 0.0).astype(BF16)
                return c
            lax.fori_loop(0, nb, body, 0)

    blk = lax.broadcasted_iota(jnp.int32, (nb, tq), 0)
    own = 2 * i + lax.broadcasted_iota(jnp.int32, (nb, tq), 1) // blk_rows
    row2 = lax.broadcasted_iota(jnp.int32, (n_feat - nb, tq), 0)
    qi = (lax.broadcasted_iota(jnp.int32, (n_feat - nb, tq), 1) % blk_rows).astype(F32)
    for hh in range(hpb):
        slope = slopes_ref[hg * hpb + hh]
        cs = slice(hh * HEAD_DIM, (hh + 1) * HEAD_DIM)
        q_t = q_ref[:, cs].astype(F32).T.astype(BF16)
        km = km_ref[:, cs]
        km_hi = km.astype(BF16)
        km_lo = (km - km_hi.astype(F32)).astype(BF16)
        g = (jnp.dot(km_hi, q_t, preferred_element_type=F32)
             + jnp.dot(km_lo, q_t, preferred_element_type=F32))
        g = jnp.where(blk < own, g, -jnp.inf)
        sel = blk == own
        for r in range(min(MOBA_TOPK, nb)):
            mx = jnp.max(g, axis=0, keepdims=True)
            idx = jnp.min(jnp.where(g == mx, blk, nb), axis=0, keepdims=True)
            pick = jnp.logical_and(blk == idx, r < own)
            sel = jnp.logical_or(sel, pick)
            g = jnp.where(pick, -jnp.inf, g)
        bias = jnp.where(sel, (-slope * blk_rows) * (own - blk).astype(F32), NEG)
        extra = jnp.where(row2 == 0, -slope * qi, jnp.where(row2 == 1, 1.0, 0.0))
        qaug_ref[hh, 0:HEAD_DIM, :] = q_t
        qaug_ref[hh, HEAD_DIM:HEAD_DIM + nb, :] = bias.astype(BF16)
        qaug_ref[hh, HEAD_DIM + nb:HEAD_DIM + n_feat, :] = extra.astype(BF16)

    n_visit = 2 * i + 2

    def block_at(t):
        return jnp.where(t < 2, 2 * i + t, jnp.minimum(t - 2, nb - 1))

    def scores(slot, b, diag=None):
        for hh in range(hpb):
            s = jnp.dot(kaug_ref[hh, b], qaug_ref[hh], preferred_element_type=F32)
            if diag is not None:
                key_pos = lax.broadcasted_iota(jnp.int32, (blk_rows, tq), 0) + diag * blk_rows
                qry_pos = lax.broadcasted_iota(jnp.int32, (blk_rows, tq), 1)
                s = jnp.where(key_pos <= qry_pos, s, NEG)
            s_ref[slot, hh] = s
            bmax_ref[slot, hh] = jnp.max(s, axis=0, keepdims=True)

    def softmax(sslot, pslot):
        for hh in range(hpb):
            m_prev = m_ref[hh]
            m_new = jnp.maximum(m_prev, bmax_ref[sslot, hh])
            alpha_ref[pslot, hh] = jnp.exp(m_prev - m_new)
            p_ref[pslot, hh] = jnp.exp(s_ref[sslot, hh] - m_new).astype(BF16)
            m_ref[hh] = m_new

    def weighted_values(pslot, b):
        for hh in range(hpb):
            acc_ref[hh] = alpha_ref[pslot, hh] * acc_ref[hh] + jnp.dot(
                vt_ref[hh, b], p_ref[pslot, hh], preferred_element_type=F32)

    m_ref[...] = jnp.full(m_ref.shape, -jnp.inf, F32)
    acc_ref[...] = jnp.zeros(acc_ref.shape, F32)
    scores(0, 2 * i, diag=0)
    scores(1, 2 * i + 1, diag=1)
    softmax(0, 1)

    def body(t, c):
        scores(0, block_at(2 * t + 2))
        softmax(1, 0)
        weighted_values(1, block_at(2 * t))
        scores(1, block_at(2 * t + 3))
        softmax(0, 1)
        weighted_values(0, block_at(2 * t + 1))
        return c
    lax.fori_loop(0, n_visit // 2, body, 0)

    for hh in range(hpb):
        o_t = acc_ref[hh, 0:HEAD_DIM, :] / acc_ref[hh, HEAD_DIM:HEAD_DIM + 1, :]
        o_ref[:, hh * HEAD_DIM:(hh + 1) * HEAD_DIM] = o_t.T.astype(o_ref.dtype)


def _attention(proj3, kmean3, slopes):
    B, S, _ = proj3.shape
    nb = S // MOBA_BLOCK
    hpb = ATTN_HEADS_PER_STEP
    tq = 2 * MOBA_BLOCK
    assert S % tq == 0 and nb % 16 == 0 and nb + 2 <= LANES and N_HEADS % hpb == 0
    A = N_HEADS * HEAD_DIM
    w = hpb * HEAD_DIM
    n_hg = N_HEADS // hpb
    grid_spec = pltpu.PrefetchScalarGridSpec(
        num_scalar_prefetch=1,
        grid=(B, n_hg, S // tq),
        in_specs=[
            pl.BlockSpec((None, tq, w), lambda b, h, i, s: (b, i, h)),
            pl.BlockSpec((None, S, w), lambda b, h, i, s: (b, 0, n_hg + h)),
            pl.BlockSpec((None, S, w), lambda b, h, i, s: (b, 0, 2 * n_hg + h)),
            pl.BlockSpec((None, nb, w), lambda b, h, i, s: (b, 0, h)),
        ],
        out_specs=pl.BlockSpec((None, tq, w), lambda b, h, i, s: (b, i, h)),
        scratch_shapes=[
            pltpu.VMEM((hpb, nb, MOBA_BLOCK, HEAD_DIM + LANES), BF16),
            pltpu.VMEM((hpb, nb, HEAD_DIM + V_PAD_ROWS, MOBA_BLOCK), BF16),
            pltpu.VMEM((hpb, HEAD_DIM + LANES, tq), BF16),
            pltpu.VMEM((2, hpb, MOBA_BLOCK, tq), F32),
            pltpu.VMEM((2, hpb, 1, tq), F32),
            pltpu.VMEM((2, hpb, MOBA_BLOCK, tq), BF16),
            pltpu.VMEM((2, hpb, 1, tq), F32),
            pltpu.VMEM((hpb, 1, tq), F32),
            pltpu.VMEM((hpb, HEAD_DIM + V_PAD_ROWS, tq), F32),
        ],
    )
    return pl.pallas_call(
        functools.partial(_attn_kernel, nb=nb, hpb=hpb, tq=tq),
        grid_spec=grid_spec,
        out_shape=jax.ShapeDtypeStruct((B, S, A), BF16),
        compiler_params=pltpu.CompilerParams(
            dimension_semantics=("arbitrary", "arbitrary", "arbitrary"),
            vmem_limit_bytes=VMEM_LIMIT, flags=ATTN_FLAGS),
        name="moba_attn",
    )(slopes, proj3, proj3, proj3, kmean3)


def _mix_kernel(o_ref, u_ref, halo_ref, band_ref, wpool_ref, beta_ref, pscale_ref, wout_ref,
                x_ref, gffn_ref, wr_hi_ref, wr_lo_ref, br_ref,
                h1_ref, f_ref, eid_ref, wts_ref, *, tm, seq):
    i = pl.program_id(0)
    t0 = (i * tm) % seq
    u = u_ref[...]
    halo = jnp.where(t0 == 0, jnp.zeros_like(halo_ref[...]), halo_ref[...])
    uext = jnp.concatenate([halo, u], axis=0)
    t = t0 + lax.broadcasted_iota(jnp.int32, (tm, 1), 0)
    zs = []
    for g, w in enumerate(POOL_WINDOWS):
        cs = slice(g * POOL_GROUP, (g + 1) * POOL_GROUP)
        wsum = jnp.dot(band_ref[g], uext[:, cs], preferred_element_type=F32)
        cnt = jnp.minimum(t + 1, w).astype(F32)
        z = wsum / cnt - u[:, cs].astype(F32)
        zs.append(jnp.dot(z.astype(BF16), wpool_ref[g], preferred_element_type=F32))
    o_pool = jnp.concatenate(zs, axis=1)
    mixed = jnp.concatenate(
        [_rms(o_ref[...].astype(F32), beta_ref[...]).astype(BF16),
         _rms(o_pool, pscale_ref[...]).astype(BF16)], axis=1)
    h1 = x_ref[...] + jnp.dot(mixed, wout_ref[...], preferred_element_type=F32)
    h1_ref[...] = h1
    f = _rms(h1, gffn_ref[...])
    f_ref[...] = f

    f_hi = f.astype(BF16)
    f_lo = (f - f_hi.astype(F32)).astype(BF16)
    wr_hi = wr_hi_ref[...]
    logits = (jnp.dot(f_hi, wr_hi, preferred_element_type=F32)
              + jnp.dot(f_lo, wr_hi, preferred_element_type=F32)
              + jnp.dot(f_hi, wr_lo_ref[...], preferred_element_type=F32))
    lt = logits.T + br_ref[...]
    row8 = lax.broadcasted_iota(jnp.int32, (N_GROUPS, tm), 0)
    gl = lt[0:N_GROUPS]
    gmax = jnp.max(gl, axis=0, keepdims=True)
    gidx = jnp.min(jnp.where(gl == gmax, row8, N_GROUPS), axis=0, keepdims=True)
    g_val = 1.0 / jnp.sum(jnp.exp(gl - gmax), axis=0, keepdims=True)
    el = jnp.zeros((EXPERTS_PER_GROUP, tm), F32)
    for g in range(N_GROUPS):
        lo = N_GROUPS + g * EXPERTS_PER_GROUP
        el = jnp.where(gidx == g, lt[lo:lo + EXPERTS_PER_GROUP], el)
    emax = jnp.max(el, axis=0, keepdims=True)
    esum = jnp.sum(jnp.exp(el - emax), axis=0, keepdims=True)
    i1 = jnp.min(jnp.where(el == emax, row8, EXPERTS_PER_GROUP), axis=0, keepdims=True)
    el2 = jnp.where(row8 == i1, -jnp.inf, el)
    m2 = jnp.max(el2, axis=0, keepdims=True)
    i2 = jnp.min(jnp.where(el2 == m2, row8, EXPERTS_PER_GROUP), axis=0, keepdims=True)
    p1 = 1.0 / esum
    p2 = jnp.exp(m2 - emax) / esum
    den = p1 + p2
    eid_ref[0:1, :] = gidx * EXPERTS_PER_GROUP + i1
    eid_ref[1:2, :] = gidx * EXPERTS_PER_GROUP + i2
    wts_ref[0:1, :] = g_val * (p1 / den)
    wts_ref[1:2, :] = g_val * (p2 / den)


def _pool_band(tm):
    r = jnp.arange(tm)[:, None]
    c = jnp.arange(POOL_HALO + tm)[None, :] - POOL_HALO
    return jnp.stack([((c <= r) & (c >= r - w + 1)).astype(BF16) for w in POOL_WINDOWS])


def _mix(o2, proj, x2, seq, w_pool, beta_attn, pool_scale, w_out, g_ffn, w_rg, b_rg, w_re, b_re):
    T, D = x2.shape
    A = N_HEADS * HEAD_DIM
    PW = len(POOL_WINDOWS) * POOL_GROUP
    tm = MIX_TM
    assert seq % tm == 0 and tm % POOL_HALO == 0 and POOL_HALO >= max(POOL_WINDOWS) - 1
    u_col = (proj.shape[1] - PW) // PW
    halo_per_tile = tm // POOL_HALO
    wr = jnp.concatenate([w_rg, jnp.transpose(w_re, (1, 0, 2)).reshape(D, N_EXPERTS)], axis=1)
    wr = jnp.pad(wr, ((0, 0), (0, LANES - wr.shape[1])))
    wr_hi = wr.astype(BF16)
    wr_lo = (wr - wr_hi.astype(F32)).astype(BF16)
    br = jnp.pad(jnp.concatenate([b_rg, b_re.reshape(-1)]), (0, LANES - N_GROUPS - N_EXPERTS))
    const = lambda shape: pl.BlockSpec(shape, lambda i: (0,) * len(shape))
    return pl.pallas_call(
        functools.partial(_mix_kernel, tm=tm, seq=seq),
        grid=(T // tm,),
        in_specs=[
            pl.BlockSpec((tm, A), lambda i: (i, 0)),
            pl.BlockSpec((tm, PW), lambda i: (i, u_col)),
            pl.BlockSpec((POOL_HALO, PW),
                         lambda i: (jnp.maximum(i * halo_per_tile - 1, 0), u_col)),
            const((len(POOL_WINDOWS), tm, POOL_HALO + tm)),
            const((len(POOL_WINDOWS), POOL_GROUP, POOL_GROUP)),
            const((1, A)), const((1, PW)), const((A + PW, D)),
            pl.BlockSpec((tm, D), lambda i: (i, 0)),
            const((1, D)), const((D, LANES)), const((D, LANES)), const((LANES, 1)),
        ],
        out_specs=[
            pl.BlockSpec((tm, D), lambda i: (i, 0)),
            pl.BlockSpec((tm, D), lambda i: (i, 0)),
            pl.BlockSpec((2, tm), lambda i: (0, i)),
            pl.BlockSpec((2, tm), lambda i: (0, i)),
        ],
        out_shape=[
            jax.ShapeDtypeStruct((T, D), F32),
            jax.ShapeDtypeStruct((T, D), F32),
            jax.ShapeDtypeStruct((2, T), jnp.int32),
            jax.ShapeDtypeStruct((2, T), F32),
        ],
        compiler_params=pltpu.CompilerParams(
            dimension_semantics=("arbitrary",), vmem_limit_bytes=VMEM_LIMIT),
        name="mix_router",
    )(o2, proj, proj, _pool_band(tm), w_pool.astype(BF16), beta_attn.reshape(1, A),
      pool_scale.reshape(1, PW), w_out.astype(BF16), x2, g_ffn.reshape(1, D),
      wr_hi, wr_lo, br.reshape(LANES, 1))


def _expert_kernel(blk_e_ref, n_used_ref, tok_ref, tok_next_ref, slot_prev_ref, w_ref,
                   wg_ref, wu_ref, wd_ref, f_hbm, y_hbm,
                   xbuf, ybuf, wg_s, wu_s, wd_s, hid_s, gsem, ssem, *, mb, n_rows):
    b = pl.program_id(0)
    n_used = n_used_ref[0]
    cur = b % 2
    oth = 1 - cur
    d_model = xbuf.shape[-1]
    d_expert = wg_s.shape[-1]

    def gather_copy(tok, r, slot):
        return pltpu.make_async_copy(f_hbm.at[pl.ds(tok, 1), :],
                                     xbuf.at[slot, pl.ds(r, 1), :], gsem.at[slot])

    def scatter_copy(dst, r, slot):
        return pltpu.make_async_copy(ybuf.at[slot, pl.ds(r, 1), :],
                                     y_hbm.at[pl.ds(dst, 1), :], ssem.at[slot])

    def start_rows(copy_fn, idx_ref, slot):
        def body(r, c):
            copy_fn(idx_ref[0, 0, r], r, slot).start()
            return c
        lax.fori_loop(0, mb, body, 0, unroll=8)

    def wait_gather(slot):
        pltpu.make_async_copy(f_hbm.at[pl.ds(0, mb), :], xbuf.at[slot], gsem.at[slot]).wait()

    def wait_scatter(slot):
        pltpu.make_async_copy(ybuf.at[slot], y_hbm.at[pl.ds(0, mb), :], ssem.at[slot]).wait()

    @pl.when(b == 0)
    def _prologue():
        start_rows(gather_copy, tok_ref, 0)
        ybuf[1] = jnp.zeros(ybuf.shape[1:], ybuf.dtype)
        for half in range(2):
            fill = pltpu.make_async_copy(
                ybuf.at[1], y_hbm.at[pl.ds(n_rows + half * mb, mb), :], ssem.at[1])
            fill.start()
            fill.wait()

    @pl.when(b < n_used)
    def _block():
        first_of_expert = jnp.logical_or(b == 0, blk_e_ref[b] != blk_e_ref[jnp.maximum(b - 1, 0)])

        @pl.when(first_of_expert)
        def _():
            wg_s[...] = wg_ref[...].astype(BF16)
            wu_s[...] = wu_ref[...].astype(BF16)
            wd_s[...] = wd_ref[...].astype(BF16)

        wait_gather(cur)
        xb = xbuf[cur].astype(BF16)
        n_hid_chunks = d_expert // EXPERT_CHUNK
        rows_per = mb // n_hid_chunks
        early_per = EXPERT_EARLY_SCATTER // n_hid_chunks
        for c in range(n_hid_chunks):
            cs = slice(c * EXPERT_CHUNK, (c + 1) * EXPERT_CHUNK)
            gate = jnp.dot(xb, wg_s[:, cs], preferred_element_type=F32)
            up = jnp.dot(xb, wu_s[:, cs], preferred_element_type=F32)
            hid_s[:, cs] = (gate * jax.nn.sigmoid(gate) * up).astype(BF16)
            for r in range(c * rows_per, (c + 1) * rows_per):
                gather_copy(tok_next_ref[0, 0, r], r, oth).start(priority=r % 2)
            for r in range(c * early_per, (c + 1) * early_per):
                scatter_copy(slot_prev_ref[0, 0, r], r, oth).start(priority=r % 2)

        @pl.when(b >= 1)
        def _():
            wait_scatter(cur)

        hid = hid_s[...]
        n_out_chunks = d_model // EXPERT_CHUNK
        rows_per = (mb - EXPERT_EARLY_SCATTER) // n_out_chunks
        for c in range(n_out_chunks):
            cs = slice(c * EXPERT_CHUNK, (c + 1) * EXPERT_CHUNK)
            y = jnp.dot(hid, wd_s[:, cs], preferred_element_type=F32)
            ybuf[cur, :, cs] = y * w_ref[...]
            for r in range(EXPERT_EARLY_SCATTER + c * rows_per,
                           EXPERT_EARLY_SCATTER + (c + 1) * rows_per):
                scatter_copy(slot_prev_ref[0, 0, r], r, oth).start(priority=r % 2)

    @pl.when(b == n_used)
    def _flush():
        wait_gather(cur)
        wait_scatter(cur)
        start_rows(scatter_copy, slot_prev_ref, oth)
        wait_scatter(oth)


def _experts(f2, eid, wts, w_gate, w_up, w_down):
    T, D = f2.shape
    DE = w_gate.shape[-1]
    mb = EXPERT_MB
    N = 2 * T
    flat_e = eid.T.reshape(N)
    flat_w = wts.T.reshape(N)
    order = jnp.argsort(flat_e, stable=True).astype(jnp.int32)
    experts = jnp.arange(N_EXPERTS, dtype=jnp.int32)
    counts = jnp.sum((flat_e[:, None] == experts[None, :]).astype(jnp.int32), axis=0)
    starts = jnp.cumsum(counts) - counts
    padded = (counts + mb - 1) // mb * mb
    pend = jnp.cumsum(padded)
    pstarts = pend - padded
    P = N + N_EXPERTS * mb
    nblk = P // mb
    last = nblk - 1
    n_used = (pend[-1] // mb).astype(jnp.int32)
    blk_start = jnp.arange(nblk, dtype=jnp.int32) * mb
    blk_e = jnp.sum((pend[None, :] <= blk_start[:, None]).astype(jnp.int32), axis=1)
    last_e = jnp.max(jnp.where(counts > 0, experts, 0))
    blk_e = jnp.minimum(blk_e, last_e)
    r = jnp.arange(mb, dtype=jnp.int32)[None, :]
    off = (blk_start - pstarts[blk_e])[:, None] + r
    valid = off < counts[blk_e][:, None]
    slot = order[jnp.where(valid, starts[blk_e][:, None] + off, 0)]
    spare_row = N + (jnp.arange(nblk, dtype=jnp.int32)[:, None] % 2) * mb + r
    tok_buf = jnp.where(valid, slot // 2, 0)
    slot_buf = jnp.where(valid, (slot % 2) * T + slot // 2, spare_row)
    w_buf = jnp.where(valid, flat_w[slot], 0.0)
    slot_prev = jnp.concatenate([N + mb + r, slot_buf], axis=0)

    grid_spec = pltpu.PrefetchScalarGridSpec(
        num_scalar_prefetch=2,
        grid=(nblk + 1,),
        in_specs=[
            pl.BlockSpec((1, 1, mb), lambda b, be, nu: (jnp.minimum(b, last), 0, 0), memory_space=pltpu.SMEM),
            pl.BlockSpec((1, 1, mb), lambda b, be, nu: (jnp.minimum(b + 1, last), 0, 0),
                         memory_space=pltpu.SMEM),
            pl.BlockSpec((1, 1, mb), lambda b, be, nu: (b, 0, 0), memory_space=pltpu.SMEM),
            pl.BlockSpec((mb, 1), lambda b, be, nu: (jnp.minimum(b, last), 0)),
            pl.BlockSpec((None, D, DE), lambda b, be, nu: (be[jnp.minimum(b, last)], 0, 0)),
            pl.BlockSpec((None, D, DE), lambda b, be, nu: (be[jnp.minimum(b, last)], 0, 0)),
            pl.BlockSpec((None, DE, D), lambda b, be, nu: (be[jnp.minimum(b, last)], 0, 0)),
            pl.BlockSpec(memory_space=pl.ANY),
        ],
        out_specs=pl.BlockSpec(memory_space=pl.ANY),
        scratch_shapes=[
            pltpu.VMEM((2, mb, D), F32),
            pltpu.VMEM((2, mb, D), F32),
            pltpu.VMEM((D, DE), BF16), pltpu.VMEM((D, DE), BF16), pltpu.VMEM((DE, D), BF16),
            pltpu.VMEM((mb, DE), BF16),
            pltpu.SemaphoreType.DMA((2,)), pltpu.SemaphoreType.DMA((2,)),
        ],
    )
    y = pl.pallas_call(
        functools.partial(_expert_kernel, mb=mb, n_rows=N),
        grid_spec=grid_spec,
        out_shape=jax.ShapeDtypeStruct((N + 2 * mb, D), F32),
        compiler_params=pltpu.CompilerParams(
            dimension_semantics=("arbitrary",), vmem_limit_bytes=VMEM_LIMIT),
        name="experts",
    )(blk_e, n_used.reshape(1), tok_buf.reshape(nblk, 1, mb), tok_buf.reshape(nblk, 1, mb),
      slot_prev.reshape(nblk + 1, 1, mb), w_buf.reshape(P, 1), w_gate, w_up, w_down, f2)
    return y


def _ple_kernel(h1_ref, y0_ref, y1_ref, p_ref, gple_ref, wg_ref, bg_ref, wple_ref, gfin_ref, out_ref,
                *, final):
    h = h1_ref[...] + y0_ref[...] + y1_ref[...]
    a = _rms(h, gple_ref[...]).astype(BF16)
    gate = jax.nn.sigmoid(jnp.dot(a, wg_ref[...], preferred_element_type=F32) + bg_ref[...])
    ple = jnp.dot(p_ref[...].astype(BF16), wple_ref[...], preferred_element_type=F32)
    h = h + gate * ple
    out_ref[...] = _rms(h, gfin_ref[...]) if final else h


def _ple(h1, y3, p2, g_ple, w_ple_gate, b_ple_gate, w_ple, g_final, final):
    T, D = h1.shape
    PD = p2.shape[1]
    tm = PLE_TM
    const = lambda shape: pl.BlockSpec(shape, lambda i: (0,) * len(shape))
    return pl.pallas_call(
        functools.partial(_ple_kernel, final=final),
        grid=(T // tm,),
        in_specs=[
            pl.BlockSpec((tm, D), lambda i: (i, 0)),
            pl.BlockSpec((tm, D), lambda i: (i, 0)),
            pl.BlockSpec((tm, D), lambda i: (T // tm + i, 0)),
            pl.BlockSpec((tm, PD), lambda i: (i, 0)),
            const((1, D)), const((D, D)), const((1, D)), const((PD, D)), const((1, D)),
        ],
        out_specs=pl.BlockSpec((tm, D), lambda i: (i, 0)),
        out_shape=jax.ShapeDtypeStruct((T, D), F32),
        compiler_params=pltpu.CompilerParams(
            dimension_semantics=("arbitrary",), vmem_limit_bytes=VMEM_LIMIT),
        name="ple_final",
    )(h1, y3, y3, p2, g_ple.reshape(1, D), w_ple_gate.astype(BF16), b_ple_gate.reshape(1, D),
      w_ple.astype(BF16), g_final.reshape(1, D))


def kernel(x, p, g_mix, w_in, beta_attn, w_pool, pool_scale, w_out, g_ffn, w_router_group,
           b_router_group, w_router_expert, b_router_expert, w_expert_gate, w_expert_up,
           w_expert_down, g_ple, w_ple, w_ple_gate, b_ple_gate, g_final):
    B, S, D = x.shape
    depth = w_in.shape[0]
    T = B * S
    A = N_HEADS * HEAD_DIM
    slopes = jnp.asarray([2.0 ** (-8.0 * (h + 1) / N_HEADS) for h in range(N_HEADS)], F32)
    h = x.reshape(T, D)
    for i in range(depth):
        proj, kmean = _inproj(h, g_mix[i], w_in[i].astype(BF16), HEAD_DIM ** -0.5)
        o = _attention(proj.reshape(B, S, -1), kmean.reshape(B, S // MOBA_BLOCK, A), slopes)
        h1, f, eid, wts = _mix(o.reshape(T, A), proj, h, S, w_pool[i], beta_attn[i], pool_scale[i],
                               w_out[i], g_ffn[i], w_router_group[i], b_router_group[i],
                               w_router_expert[i], b_router_expert[i])
        y3 = _experts(f, eid, wts, w_expert_gate[i], w_expert_up[i], w_expert_down[i])
        h = _ple(h1, y3, p[i].reshape(T, -1), g_ple[i], w_ple_gate[i], b_ple_gate[i], w_ple[i],
                 g_final, final=(i == depth - 1))
    return h.reshape(B, S, D)
```

```python
import functools

import jax
import jax.numpy as jnp
from jax import lax
from jax.experimental import pallas as pl
from jax.experimental.pallas import tpu as pltpu

F32 = jnp.float32
BF16 = jnp.bfloat16

N_HEADS = 8
HEAD_DIM = 128
MOBA_BLOCK = 256
MOBA_TOPK = 3
POOL_WINDOWS = (2, 4, 8, 16)
POOL_GROUP = 256
N_GROUPS = 8
EXPERTS_PER_GROUP = 8
N_EXPERTS = N_GROUPS * EXPERTS_PER_GROUP
EPS = 1e-6

LANES = 128
NEG = -1e30
VMEM_LIMIT = 56 * 1024 * 1024

INPROJ_TM = 1024
MIX_TM = 256
POOL_HALO = 128
EXPERT_MB = 256
EXPERT_CHUNK = 256
EXPERT_EARLY_SCATTER = 64
PLE_TM = 256
ATTN_HEADS_PER_STEP = 2
V_PAD_ROWS = 16
ATTN_FLAGS = None

_NT = (((1,), (1,)), ((), ()))


def _rms(x, gain):
    return x * lax.rsqrt(jnp.mean(x * x, axis=-1, keepdims=True) + EPS) * gain


def _inproj_kernel(x_ref, g_ref, w_ref, proj_ref, kmean_ref, a_ref, *, tm, scale):
    j = pl.program_id(1)

    @pl.when(j == 0)
    def _():
        a_ref[...] = _rms(x_ref[...], g_ref[...]).astype(BF16)

    acc = jnp.dot(a_ref[...], w_ref[...], preferred_element_type=F32)
    proj_ref[...] = (acc * jnp.where(j == 0, scale, 1.0)).astype(BF16)

    @pl.when(j == 1)
    def _():
        kmean_ref[0] = jnp.mean(acc.reshape(tm // MOBA_BLOCK, MOBA_BLOCK, acc.shape[-1]), axis=1)


def _inproj(x2, g_mix, w_in_bf16, scale):
    T, D = x2.shape
    A = N_HEADS * HEAD_DIM
    tm = INPROJ_TM
    n_chunks = w_in_bf16.shape[1] // A
    return pl.pallas_call(
        functools.partial(_inproj_kernel, tm=tm, scale=scale),
        grid=(T // tm, n_chunks),
        in_specs=[
            pl.BlockSpec((tm, D), lambda i, j: (i, 0)),
            pl.BlockSpec((1, D), lambda i, j: (0, 0)),
            pl.BlockSpec((D, A), lambda i, j: (0, j)),
        ],
        out_specs=[
            pl.BlockSpec((tm, A), lambda i, j: (i, j)),
            pl.BlockSpec((1, tm // MOBA_BLOCK, A), lambda i, j: (i, 0, 0)),
        ],
        out_shape=[
            jax.ShapeDtypeStruct((T, n_chunks * A), BF16),
            jax.ShapeDtypeStruct((T // tm, tm // MOBA_BLOCK, A), F32),
        ],
        scratch_shapes=[pltpu.VMEM((tm, D), BF16)],
        compiler_params=pltpu.CompilerParams(
            dimension_semantics=("arbitrary", "arbitrary"), vmem_limit_bytes=VMEM_LIMIT),
        name="inproj",
    )(x2, g_mix.reshape(1, D), w_in_bf16)


def _attn_kernel(slopes_ref, q_ref, k_ref, v_ref, km_ref, o_ref,
                 kaug_ref, vt_ref, qaug_ref, s_ref, bmax_ref, p_ref, alpha_ref, m_ref, acc_ref,
                 *, nb, hpb, tq):
    hg = pl.program_id(1)
    i = pl.program_id(2)
    blk_rows = MOBA_BLOCK
    n_feat = LANES

    @pl.when(i == 0)
    def _build_kv():
        for hh in range(hpb):
            slope = slopes_ref[hg * hpb + hh]
            cs = slice(hh * HEAD_DIM, (hh + 1) * HEAD_DIM)

            def body(j, c, hh=hh, slope=slope, cs=cs):
                r0 = pl.multiple_of(j * blk_rows, blk_rows)
                kaug_ref[hh, j, :, 0:HEAD_DIM] = k_ref[pl.ds(r0, blk_rows), cs]
                lane = lax.broadcasted_iota(jnp.int32, (blk_rows, n_feat), 1)
                kj = lax.broadcasted_iota(jnp.int32, (blk_rows, n_feat), 0).astype(F32)
                feat = jnp.where(lane == j, 1.0,
                                 jnp.where(lane == nb, 1.0,
                                           jnp.where(lane == nb + 1, slope * kj, 0.0)))
                kaug_ref[hh, j, :, HEAD_DIM:HEAD_DIM + n_feat] = feat.astype(BF16)
                vt_ref[hh, j, 0:HEAD_DIM, :] = v_ref[pl.ds(r0, blk_rows), cs].astype(F32).T.astype(BF16)
                ones_row = lax.broadcasted_iota(jnp.int32, (V_PAD_ROWS, blk_rows), 0) == 0
                vt_ref[hh, j, HEAD_DIM:HEAD_DIM + V_PAD_ROWS, :] = jnp.where(ones_row, 1.0, 0.0).astype(BF16)
                return c
            lax.fori_loop(0, nb, body, 0)

    blk = lax.broadcasted_iota(jnp.int32, (nb, tq), 0)
    own = 2 * i + lax.broadcasted_iota(jnp.int32, (nb, tq), 1) // blk_rows
    row2 = lax.broadcasted_iota(jnp.int32, (n_feat - nb, tq), 0)
    qi = (lax.broadcasted_iota(jnp.int32, (n_feat - nb, tq), 1) % blk_rows).astype(F32)
    for hh in range(hpb):
        slope = slopes_ref[hg * hpb + hh]
        cs = slice(hh * HEAD_DIM, (hh + 1) * HEAD_DIM)
        q_t = q_ref[:, cs].astype(F32).T.astype(BF16)
        km = km_ref[:, cs]
        km_hi = km.astype(BF16)
        km_lo = (km - km_hi.astype(F32)).astype(BF16)
        g = (jnp.dot(km_hi, q_t, preferred_element_type=F32)
             + jnp.dot(km_lo, q_t, preferred_element_type=F32))
        g = jnp.where(blk < own, g, -jnp.inf)
        sel = blk == own
        for r in range(min(MOBA_TOPK, nb)):
            mx = jnp.max(g, axis=0, keepdims=True)
            idx = jnp.min(jnp.where(g == mx, blk, nb), axis=0, keepdims=True)
            pick = jnp.logical_and(blk == idx, r < own)
            sel = jnp.logical_or(sel, pick)
            g = jnp.where(pick, -jnp.inf, g)
        bias = jnp.where(sel, (-slope * blk_rows) * (own - blk).astype(F32), NEG)
        extra = jnp.where(row2 == 0, -slope * qi, jnp.where(row2 == 1, 1.0, 0.0))
        qaug_ref[hh, 0:HEAD_DIM, :] = q_t
        qaug_ref[hh, HEAD_DIM:HEAD_DIM + nb, :] = bias.astype(BF16)
        qaug_ref[hh, HEAD_DIM + nb:HEAD_DIM + n_feat, :] = extra.astype(BF16)

    n_visit = 2 * i + 2

    def block_at(t):
        return jnp.where(t < 2, 2 * i + t, jnp.minimum(t - 2, nb - 1))

    def scores(slot, b, diag=None):
        for hh in range(hpb):
            s = jnp.dot(kaug_ref[hh, b], qaug_ref[hh], preferred_element_type=F32)
            if diag is not None:
                key_pos = lax.broadcasted_iota(jnp.int32, (blk_rows, tq), 0) + diag * blk_rows
                qry_pos = lax.broadcasted_iota(jnp.int32, (blk_rows, tq), 1)
                s = jnp.where(key_pos <= qry_pos, s, NEG)
            s_ref[slot, hh] = s
            bmax_ref[slot, hh] = jnp.max(s, axis=0, keepdims=True)

    def softmax(sslot, pslot):
        for hh in range(hpb):
            for c0 in range(0, tq, blk_rows):
                cq = slice(c0, c0 + blk_rows)
                m_prev = m_ref[hh, :, cq]
                m_new = jnp.maximum(m_prev, bmax_ref[sslot, hh, :, cq])
                alpha_ref[pslot, hh, :, cq] = jnp.exp(m_prev - m_new)
                p_ref[pslot, hh, :, cq] = jnp.exp(s_ref[sslot, hh, :, cq] - m_new).astype(BF16)
                m_ref[hh, :, cq] = m_new

    def weighted_values(pslot, b):
        for hh in range(hpb):
            acc_ref[hh] = alpha_ref[pslot, hh] * acc_ref[hh] + jnp.dot(
                vt_ref[hh, b], p_ref[pslot, hh], preferred_element_type=F32)

    m_ref[...] = jnp.full(m_ref.shape, -jnp.inf, F32)
    acc_ref[...] = jnp.zeros(acc_ref.shape, F32)
    scores(0, 2 * i, diag=0)
    scores(1, 2 * i + 1, diag=1)
    softmax(0, 1)

    def body(t, c):
        scores(0, block_at(2 * t + 2))
        softmax(1, 0)
        weighted_values(1, block_at(2 * t))
        scores(1, block_at(2 * t + 3))
        softmax(0, 1)
        weighted_values(0, block_at(2 * t + 1))
        return c
    lax.fori_loop(0, n_visit // 2, body, 0)

    for hh in range(hpb):
        o_t = acc_ref[hh, 0:HEAD_DIM, :] / acc_ref[hh, HEAD_DIM:HEAD_DIM + 1, :]
        o_ref[:, hh * HEAD_DIM:(hh + 1) * HEAD_DIM] = o_t.T.astype(o_ref.dtype)


def _attention(proj3, kmean3, slopes):
    B, S, _ = proj3.shape
    nb = S // MOBA_BLOCK
    hpb = ATTN_HEADS_PER_STEP
    tq = 2 * MOBA_BLOCK
    assert S % tq == 0 and nb % 16 == 0 and nb + 2 <= LANES and N_HEADS % hpb == 0
    A = N_HEADS * HEAD_DIM
    w = hpb * HEAD_DIM
    n_hg = N_HEADS // hpb
    grid_spec = pltpu.PrefetchScalarGridSpec(
        num_scalar_prefetch=1,
        grid=(B, n_hg, S // tq),
        in_specs=[
            pl.BlockSpec((None, tq, w), lambda b, h, i, s: (b, i, h)),
            pl.BlockSpec((None, S, w), lambda b, h, i, s: (b, 0, n_hg + h)),
            pl.BlockSpec((None, S, w), lambda b, h, i, s: (b, 0, 2 * n_hg + h)),
            pl.BlockSpec((None, nb, w), lambda b, h, i, s: (b, 0, h)),
        ],
        out_specs=pl.BlockSpec((None, tq, w), lambda b, h, i, s: (b, i, h)),
        scratch_shapes=[
            pltpu.VMEM((hpb, nb, MOBA_BLOCK, HEAD_DIM + LANES), BF16),
            pltpu.VMEM((hpb, nb, HEAD_DIM + V_PAD_ROWS, MOBA_BLOCK), BF16),
            pltpu.VMEM((hpb, HEAD_DIM + LANES, tq), BF16),
            pltpu.VMEM((2, hpb, MOBA_BLOCK, tq), F32),
            pltpu.VMEM((2, hpb, 1, tq), F32),
            pltpu.VMEM((2, hpb, MOBA_BLOCK, tq), BF16),
            pltpu.VMEM((2, hpb, 1, tq), F32),
            pltpu.VMEM((hpb, 1, tq), F32),
            pltpu.VMEM((hpb, HEAD_DIM + V_PAD_ROWS, tq), F32),
        ],
    )
    return pl.pallas_call(
        functools.partial(_attn_kernel, nb=nb, hpb=hpb, tq=tq),
        grid_spec=grid_spec,
        out_shape=jax.ShapeDtypeStruct((B, S, A), BF16),
        compiler_params=pltpu.CompilerParams(
            dimension_semantics=("arbitrary", "arbitrary", "arbitrary"),
            vmem_limit_bytes=VMEM_LIMIT, flags=ATTN_FLAGS),
        name="moba_attn",
    )(slopes, proj3, proj3, proj3, kmean3)


def _mix_kernel(o_ref, u_ref, halo_ref, band_ref, wpool_ref, beta_ref, pscale_ref, wout_ref,
                x_ref, gffn_ref, wr_hi_ref, wr_lo_ref, br_ref,
                h1_ref, f_ref, eid_ref, wts_ref, *, tm, seq):
    i = pl.program_id(0)
    t0 = (i * tm) % seq
    u = u_ref[...]
    halo = jnp.where(t0 == 0, jnp.zeros_like(halo_ref[...]), halo_ref[...])
    uext = jnp.concatenate([halo, u], axis=0)
    t = t0 + lax.broadcasted_iota(jnp.int32, (tm, 1), 0)
    zs = []
    for g, w in enumerate(POOL_WINDOWS):
        cs = slice(g * POOL_GROUP, (g + 1) * POOL_GROUP)
        wsum = jnp.dot(band_ref[g], uext[:, cs], preferred_element_type=F32)
        cnt = jnp.minimum(t + 1, w).astype(F32)
        z = wsum / cnt - u[:, cs].astype(F32)
        zs.append(jnp.dot(z.astype(BF16), wpool_ref[g], preferred_element_type=F32))
    o_pool = jnp.concatenate(zs, axis=1)
    mixed = jnp.concatenate(
        [_rms(o_ref[...].astype(F32), beta_ref[...]).astype(BF16),
         _rms(o_pool, pscale_ref[...]).astype(BF16)], axis=1)
    h1 = x_ref[...] + jnp.dot(mixed, wout_ref[...], preferred_element_type=F32)
    h1_ref[...] = h1
    f = _rms(h1, gffn_ref[...])
    f_ref[...] = f

    f_hi = f.astype(BF16)
    f_lo = (f - f_hi.astype(F32)).astype(BF16)
    wr_hi = wr_hi_ref[...]
    logits = (jnp.dot(f_hi, wr_hi, preferred_element_type=F32)
              + jnp.dot(f_lo, wr_hi, preferred_element_type=F32)
              + jnp.dot(f_hi, wr_lo_ref[...], preferred_element_type=F32))
    lt = logits.T + br_ref[...]
    row8 = lax.broadcasted_iota(jnp.int32, (N_GROUPS, tm), 0)
    gl = lt[0:N_GROUPS]
    gmax = jnp.max(gl, axis=0, keepdims=True)
    gidx = jnp.min(jnp.where(gl == gmax, row8, N_GROUPS), axis=0, keepdims=True)
    g_val = 1.0 / jnp.sum(jnp.exp(gl - gmax), axis=0, keepdims=True)
    el = jnp.zeros((EXPERTS_PER_GROUP, tm), F32)
    for g in range(N_GROUPS):
        lo = N_GROUPS + g * EXPERTS_PER_GROUP
        el = jnp.where(gidx == g, lt[lo:lo + EXPERTS_PER_GROUP], el)
    emax = jnp.max(el, axis=0, keepdims=True)
    esum = jnp.sum(jnp.exp(el - emax), axis=0, keepdims=True)
    i1 = jnp.min(jnp.where(el == emax, row8, EXPERTS_PER_GROUP), axis=0, keepdims=True)
    el2 = jnp.where(row8 == i1, -jnp.inf, el)
    m2 = jnp.max(el2, axis=0, keepdims=True)
    i2 = jnp.min(jnp.where(el2 == m2, row8, EXPERTS_PER_GROUP), axis=0, keepdims=True)
    p1 = 1.0 / esum
    p2 = jnp.exp(m2 - emax) / esum
    den = p1 + p2
    eid_ref[0:1, :] = gidx * EXPERTS_PER_GROUP + i1
    eid_ref[1:2, :] = gidx * EXPERTS_PER_GROUP + i2
    wts_ref[0:1, :] = g_val * (p1 / den)
    wts_ref[1:2, :] = g_val * (p2 / den)


def _pool_band(tm):
    r = jnp.arange(tm)[:, None]
    c = jnp.arange(POOL_HALO + tm)[None, :] - POOL_HALO
    return jnp.stack([((c <= r) & (c >= r - w + 1)).astype(BF16) for w in POOL_WINDOWS])


def _mix(o2, proj, x2, seq, w_pool, beta_attn, pool_scale, w_out, g_ffn, w_rg, b_rg, w_re, b_re):
    T, D = x2.shape
    A = N_HEADS * HEAD_DIM
    PW = len(POOL_WINDOWS) * POOL_GROUP
    tm = MIX_TM
    assert seq % tm == 0 and tm % POOL_HALO == 0 and POOL_HALO >= max(POOL_WINDOWS) - 1
    u_col = (proj.shape[1] - PW) // PW
    halo_per_tile = tm // POOL_HALO
    wr = jnp.concatenate([w_rg, jnp.transpose(w_re, (1, 0, 2)).reshape(D, N_EXPERTS)], axis=1)
    wr = jnp.pad(wr, ((0, 0), (0, LANES - wr.shape[1])))
    wr_hi = wr.astype(BF16)
    wr_lo = (wr - wr_hi.astype(F32)).astype(BF16)
    br = jnp.pad(jnp.concatenate([b_rg, b_re.reshape(-1)]), (0, LANES - N_GROUPS - N_EXPERTS))
    const = lambda shape: pl.BlockSpec(shape, lambda i: (0,) * len(shape))
    return pl.pallas_call(
        functools.partial(_mix_kernel, tm=tm, seq=seq),
        grid=(T // tm,),
        in_specs=[
            pl.BlockSpec((tm, A), lambda i: (i, 0)),
            pl.BlockSpec((tm, PW), lambda i: (i, u_col)),
            pl.BlockSpec((POOL_HALO, PW),
                         lambda i: (jnp.maximum(i * halo_per_tile - 1, 0), u_col)),
            const((len(POOL_WINDOWS), tm, POOL_HALO + tm)),
            const((len(POOL_WINDOWS), POOL_GROUP, POOL_GROUP)),
            const((1, A)), const((1, PW)), const((A + PW, D)),
            pl.BlockSpec((tm, D), lambda i: (i, 0)),
            const((1, D)), const((D, LANES)), const((D, LANES)), const((LANES, 1)),
        ],
        out_specs=[
            pl.BlockSpec((tm, D), lambda i: (i, 0)),
            pl.BlockSpec((tm, D), lambda i: (i, 0)),
            pl.BlockSpec((2, tm), lambda i: (0, i)),
            pl.BlockSpec((2, tm), lambda i: (0, i)),
        ],
        out_shape=[
            jax.ShapeDtypeStruct((T, D), F32),
            jax.ShapeDtypeStruct((T, D), F32),
            jax.ShapeDtypeStruct((2, T), jnp.int32),
            jax.ShapeDtypeStruct((2, T), F32),
        ],
        compiler_params=pltpu.CompilerParams(
            dimension_semantics=("arbitrary",), vmem_limit_bytes=VMEM_LIMIT),
        name="mix_router",
    )(o2, proj, proj, _pool_band(tm), w_pool.astype(BF16), beta_attn.reshape(1, A),
      pool_scale.reshape(1, PW), w_out.astype(BF16), x2, g_ffn.reshape(1, D),
      wr_hi, wr_lo, br.reshape(LANES, 1))


def _expert_kernel(blk_e_ref, n_used_ref, tok_ref, tok_next_ref, slot_prev_ref, w_ref,
                   wg_ref, wu_ref, wd_ref, f_hbm, y_hbm,
                   xbuf, ybuf, wg_s, wu_s, wd_s, hid_s, gsem, ssem, *, mb, n_rows):
    b = pl.program_id(0)
    n_used = n_used_ref[0]
    cur = b % 2
    oth = 1 - cur
    d_model = xbuf.shape[-1]
    d_expert = wg_s.shape[-1]

    def gather_copy(tok, r, slot):
        return pltpu.make_async_copy(f_hbm.at[pl.ds(tok, 1), :],
                                     xbuf.at[slot, pl.ds(r, 1), :], gsem.at[slot])

    def scatter_copy(dst, r, slot):
        return pltpu.make_async_copy(ybuf.at[slot, pl.ds(r, 1), :],
                                     y_hbm.at[pl.ds(dst, 1), :], ssem.at[slot])

    def start_rows(copy_fn, idx_ref, slot):
        def body(r, c):
            copy_fn(idx_ref[0, 0, r], r, slot).start()
            return c
        lax.fori_loop(0, mb, body, 0, unroll=8)

    def wait_gather(slot):
        pltpu.make_async_copy(f_hbm.at[pl.ds(0, mb), :], xbuf.at[slot], gsem.at[slot]).wait()

    def wait_scatter(slot):
        pltpu.make_async_copy(ybuf.at[slot], y_hbm.at[pl.ds(0, mb), :], ssem.at[slot]).wait()

    @pl.when(b == 0)
    def _prologue():
        start_rows(gather_copy, tok_ref, 0)
        ybuf[1] = jnp.zeros(ybuf.shape[1:], ybuf.dtype)
        for half in range(2):
            fill = pltpu.make_async_copy(
                ybuf.at[1], y_hbm.at[pl.ds(n_rows + half * mb, mb), :], ssem.at[1])
            fill.start()
            fill.wait()

    @pl.when(b < n_used)
    def _block():
        first_of_expert = jnp.logical_or(b == 0, blk_e_ref[b] != blk_e_ref[jnp.maximum(b - 1, 0)])

        @pl.when(first_of_expert)
        def _():
            wg_s[...] = wg_ref[...].astype(BF16)
            wu_s[...] = wu_ref[...].astype(BF16)
            wd_s[...] = wd_ref[...].astype(BF16)

        wait_gather(cur)
        xb = xbuf[cur].astype(BF16)
        n_hid_chunks = d_expert // EXPERT_CHUNK
        rows_per = mb // n_hid_chunks
        early_per = EXPERT_EARLY_SCATTER // n_hid_chunks
        for c in range(n_hid_chunks):
            cs = slice(c * EXPERT_CHUNK, (c + 1) * EXPERT_CHUNK)
            gate = jnp.dot(xb, wg_s[:, cs], preferred_element_type=F32)
            up = jnp.dot(xb, wu_s[:, cs], preferred_element_type=F32)
            hid_s[:, cs] = (gate * jax.nn.sigmoid(gate) * up).astype(BF16)
            for r in range(c * rows_per, (c + 1) * rows_per):
                gather_copy(tok_next_ref[0, 0, r], r, oth).start(priority=r % 2)
            for r in range(c * early_per, (c + 1) * early_per):
                scatter_copy(slot_prev_ref[0, 0, r], r, oth).start(priority=r % 2)

        @pl.when(b >= 1)
        def _():
            wait_scatter(cur)

        hid = hid_s[...]
        n_out_chunks = d_model // EXPERT_CHUNK
        rows_per = (mb - EXPERT_EARLY_SCATTER) // n_out_chunks
        for c in range(n_out_chunks):
            cs = slice(c * EXPERT_CHUNK, (c + 1) * EXPERT_CHUNK)
            y = jnp.dot(hid, wd_s[:, cs], preferred_element_type=F32)
            ybuf[cur, :, cs] = y * w_ref[...]
            for r in range(EXPERT_EARLY_SCATTER + c * rows_per,
                           EXPERT_EARLY_SCATTER + (c + 1) * rows_per):
                scatter_copy(slot_prev_ref[0, 0, r], r, oth).start(priority=r % 2)

    @pl.when(b == n_used)
    def _flush():
        wait_gather(cur)
        wait_scatter(cur)
        start_rows(scatter_copy, slot_prev_ref, oth)
        wait_scatter(oth)


def _experts(f2, eid, wts, w_gate, w_up, w_down):
    T, D = f2.shape
    DE = w_gate.shape[-1]
    mb = EXPERT_MB
    N = 2 * T
    flat_e = eid.T.reshape(N)
    flat_w = wts.T.reshape(N)
    order = jnp.argsort(flat_e, stable=True).astype(jnp.int32)
    experts = jnp.arange(N_EXPERTS, dtype=jnp.int32)
    counts = jnp.sum((flat_e[:, None] == experts[None, :]).astype(jnp.int32), axis=0)
    starts = jnp.cumsum(counts) - counts
    padded = (counts + mb - 1) // mb * mb
    pend = jnp.cumsum(padded)
    pstarts = pend - padded
    P = N + N_EXPERTS * mb
    nblk = P // mb
    last = nblk - 1
    n_used = (pend[-1] // mb).astype(jnp.int32)
    blk_start = jnp.arange(nblk, dtype=jnp.int32) * mb
    blk_e = jnp.sum((pend[None, :] <= blk_start[:, None]).astype(jnp.int32), axis=1)
    last_e = jnp.max(jnp.where(counts > 0, experts, 0))
    blk_e = jnp.minimum(blk_e, last_e)
    r = jnp.arange(mb, dtype=jnp.int32)[None, :]
    off = (blk_start - pstarts[blk_e])[:, None] + r
    valid = off < counts[blk_e][:, None]
    slot = order[jnp.where(valid, starts[blk_e][:, None] + off, 0)]
    spare_row = N + (jnp.arange(nblk, dtype=jnp.int32)[:, None] % 2) * mb + r
    tok_buf = jnp.where(valid, slot // 2, 0)
    slot_buf = jnp.where(valid, (slot % 2) * T + slot // 2, spare_row)
    w_buf = jnp.where(valid, flat_w[slot], 0.0)
    slot_prev = jnp.concatenate([N + mb + r, slot_buf], axis=0)

    grid_spec = pltpu.PrefetchScalarGridSpec(
        num_scalar_prefetch=2,
        grid=(nblk + 1,),
        in_specs=[
            pl.BlockSpec((1, 1, mb), lambda b, be, nu: (jnp.minimum(b, last), 0, 0), memory_space=pltpu.SMEM),
            pl.BlockSpec((1, 1, mb), lambda b, be, nu: (jnp.minimum(b + 1, last), 0, 0),
                         memory_space=pltpu.SMEM),
            pl.BlockSpec((1, 1, mb), lambda b, be, nu: (b, 0, 0), memory_space=pltpu.SMEM),
            pl.BlockSpec((mb, 1), lambda b, be, nu: (jnp.minimum(b, last), 0)),
            pl.BlockSpec((None, D, DE), lambda b, be, nu: (be[jnp.minimum(b, last)], 0, 0)),
            pl.BlockSpec((None, D, DE), lambda b, be, nu: (be[jnp.minimum(b, last)], 0, 0)),
            pl.BlockSpec((None, DE, D), lambda b, be, nu: (be[jnp.minimum(b, last)], 0, 0)),
            pl.BlockSpec(memory_space=pl.ANY),
        ],
        out_specs=pl.BlockSpec(memory_space=pl.ANY),
        scratch_shapes=[
            pltpu.VMEM((2, mb, D), F32),
            pltpu.VMEM((2, mb, D), F32),
            pltpu.VMEM((D, DE), BF16), pltpu.VMEM((D, DE), BF16), pltpu.VMEM((DE, D), BF16),
            pltpu.VMEM((mb, DE), BF16),
            pltpu.SemaphoreType.DMA((2,)), pltpu.SemaphoreType.DMA((2,)),
        ],
    )
    y = pl.pallas_call(
        functools.partial(_expert_kernel, mb=mb, n_rows=N),
        grid_spec=grid_spec,
        out_shape=jax.ShapeDtypeStruct((N + 2 * mb, D), F32),
        compiler_params=pltpu.CompilerParams(
            dimension_semantics=("arbitrary",), vmem_limit_bytes=VMEM_LIMIT),
        name="experts",
    )(blk_e, n_used.reshape(1), tok_buf.reshape(nblk, 1, mb), tok_buf.reshape(nblk, 1, mb),
      slot_prev.reshape(nblk + 1, 1, mb), w_buf.reshape(P, 1), w_gate, w_up, w_down, f2)
    return y


def _ple_kernel(h1_ref, y0_ref, y1_ref, p_ref, gple_ref, wg_ref, bg_ref, wple_ref, gfin_ref, out_ref,
                *, final):
    h = h1_ref[...] + y0_ref[...] + y1_ref[...]
    a = _rms(h, gple_ref[...]).astype(BF16)
    gate = jax.nn.sigmoid(jnp.dot(a, wg_ref[...], preferred_element_type=F32) + bg_ref[...])
    ple = jnp.dot(p_ref[...].astype(BF16), wple_ref[...], preferred_element_type=F32)
    h = h + gate * ple
    out_ref[...] = _rms(h, gfin_ref[...]) if final else h


def _ple(h1, y3, p2, g_ple, w_ple_gate, b_ple_gate, w_ple, g_final, final):
    T, D = h1.shape
    PD = p2.shape[1]
    tm = PLE_TM
    const = lambda shape: pl.BlockSpec(shape, lambda i: (0,) * len(shape))
    return pl.pallas_call(
        functools.partial(_ple_kernel, final=final),
        grid=(T // tm,),
        in_specs=[
            pl.BlockSpec((tm, D), lambda i: (i, 0)),
            pl.BlockSpec((tm, D), lambda i: (i, 0)),
            pl.BlockSpec((tm, D), lambda i: (T // tm + i, 0)),
            pl.BlockSpec((tm, PD), lambda i: (i, 0)),
            const((1, D)), const((D, D)), const((1, D)), const((PD, D)), const((1, D)),
        ],
        out_specs=pl.BlockSpec((tm, D), lambda i: (i, 0)),
        out_shape=jax.ShapeDtypeStruct((T, D), F32),
        compiler_params=pltpu.CompilerParams(
            dimension_semantics=("arbitrary",), vmem_limit_bytes=VMEM_LIMIT),
        name="ple_final",
    )(h1, y3, y3, p2, g_ple.reshape(1, D), w_ple_gate.astype(BF16), b_ple_gate.reshape(1, D),
      w_ple.astype(BF16), g_final.reshape(1, D))


def kernel(x, p, g_mix, w_in, beta_attn, w_pool, pool_scale, w_out, g_ffn, w_router_group,
           b_router_group, w_router_expert, b_router_expert, w_expert_gate, w_expert_up,
           w_expert_down, g_ple, w_ple, w_ple_gate, b_ple_gate, g_final):
    B, S, D = x.shape
    depth = w_in.shape[0]
    T = B * S
    A = N_HEADS * HEAD_DIM
    slopes = jnp.asarray([2.0 ** (-8.0 * (h + 1) / N_HEADS) for h in range(N_HEADS)], F32)
    h = x.reshape(T, D)
    for i in range(depth):
        proj, kmean = _inproj(h, g_mix[i], w_in[i].astype(BF16), HEAD_DIM ** -0.5)
        o = _attention(proj.reshape(B, S, -1), kmean.reshape(B, S // MOBA_BLOCK, A), slopes)
        h1, f, eid, wts = _mix(o.reshape(T, A), proj, h, S, w_pool[i], beta_attn[i], pool_scale[i],
                               w_out[i], g_ffn[i], w_router_group[i], b_router_group[i],
                               w_router_expert[i], b_router_expert[i])
        y3 = _experts(f, eid, wts, w_expert_gate[i], w_expert_up[i], w_expert_down[i])
        h = _ple(h1, y3, p[i].reshape(T, -1), g_ple[i], w_ple_gate[i], b_ple_gate[i], w_ple[i],
                 g_final, final=(i == depth - 1))
    return h.reshape(B, S, D)
```
